```python
import math
import jax, jax.numpy as jnp
from jax import lax
import numpy as np

D_MODEL = 2048
BATCH = 1
SEQ = 8192
DEPTH = 4

PLE_DIM = 256
D_A = D_MODEL // 2
A_GROUPS = 8
CONV_A_WIDTH = 31
D_B = D_MODEL // 2
RG_HEADS = 8
RG_HEAD_DIM = D_B // RG_HEADS
CONV_B_WIDTH = 4
RG_C = 8.0
SB_HEADS = 16
SB_HEAD_DIM = D_MODEL // SB_HEADS
BLOCK_Q = 128
D_FF = 4 * D_MODEL
N_EVEN = (DEPTH + 1) // 2
N_ODD = DEPTH // 2
EPS = 1e-6
D_IN_REC = 2 * D_A + 2 * D_B

kernel_name = "hybrid_conformer_rglru_stickbreaking_trunk"


def rms_norm(x, g):
    xf = x.astype(jnp.float32)
    y = xf * lax.rsqrt(jnp.mean(xf * xf, axis=-1, keepdims=True) + EPS)
    return (y * g.astype(jnp.float32)).astype(x.dtype)


def layer_norm(x, g, b):
    xf = x.astype(jnp.float32)
    mu = jnp.mean(xf, axis=-1, keepdims=True)
    var = jnp.mean(jnp.square(xf - mu), axis=-1, keepdims=True)
    y = (xf - mu) * lax.rsqrt(var + EPS)
    return (y * g.astype(jnp.float32) + b.astype(jnp.float32)).astype(x.dtype)


def causal_depthwise_conv(x, w, b):
    k_width = w.shape[0]
    out = lax.conv_general_dilated(
        x, w[:, None, :].astype(x.dtype), window_strides=(1,), padding=[(k_width - 1, 0)],
        dimension_numbers=("NWC", "WIO", "NWC"), feature_group_count=x.shape[-1])
    return out + b


def rg_lru(x, w_a, b_a, w_x, b_x, lam):
    bsz, seq, ch = x.shape
    xh = x.reshape(bsz, seq, RG_HEADS, RG_HEAD_DIM)
    r = jax.nn.sigmoid(jnp.einsum("bshi,hij->bshj", xh, w_a).reshape(bsz, seq, ch) + b_a)
    i = jax.nn.sigmoid(jnp.einsum("bshi,hij->bshj", xh, w_x).reshape(bsz, seq, ch) + b_x)
    log_a = -RG_C * r.astype(jnp.float32) * jax.nn.softplus(-lam.astype(jnp.float32))
    a = jnp.exp(log_a)
    mult = jnp.sqrt(-jnp.expm1(2.0 * log_a))
    u = mult * (i * x).astype(jnp.float32)

    def combine(c1, c2):
        a1, u1 = c1
        a2, u2 = c2
        return a1 * a2, a2 * u1 + u2

    _, h = lax.associative_scan(combine, (a, u), axis=1)
    return h.astype(x.dtype)


def conv_recurrent_mixer(hn, w_in, conv_a_w, conv_a_b, ln_a_g, ln_a_b,
                         conv_b_w, conv_b_b, w_rg_a, b_rg_a, w_rg_x, b_rg_x, rg_lambda, w_out):
    u = hn @ w_in
    a_val, a_gate, xr, gr = jnp.split(u, [D_A, 2 * D_A, 2 * D_A + D_B], axis=-1)
    ya = a_val * jax.nn.sigmoid(a_gate)
    ya = causal_depthwise_conv(ya, conv_a_w, conv_a_b)
    ya = jax.nn.silu(layer_norm(ya, ln_a_g, ln_a_b))
    xr = causal_depthwise_conv(xr, conv_b_w, conv_b_b)
    yb = rg_lru(xr, w_rg_a, b_rg_a, w_rg_x, b_rg_x, rg_lambda) * jax.nn.gelu(gr)
    return jnp.concatenate([ya, yb], axis=-1) @ w_out


def stick_breaking_attention(q, k, v):
    bsz, nh, seq, dh = q.shape
    n_blocks = seq // BLOCK_Q
    scale = 1.0 / math.sqrt(dh)
    q_blocks = q.reshape(bsz, nh, n_blocks, BLOCK_Q, dh).transpose(2, 0, 1, 3, 4)
    key_pos = jnp.arange(seq)

    def one_block(args):
        q_blk, blk_idx = args
        z = jnp.einsum("bhqd,bhkd->bhqk", q_blk, k,
                       preferred_element_type=jnp.float32) * scale
        q_pos = blk_idx * BLOCK_Q + jnp.arange(BLOCK_Q)
        mask = key_pos[None, :] < q_pos[:, None]
        log_one_minus = jnp.where(mask, -jax.nn.softplus(z), 0.0)
        key_axis = log_one_minus.ndim - 1
        tail = lax.cumsum(log_one_minus, axis=key_axis, reverse=True) - log_one_minus
        log_w = jax.nn.log_sigmoid(z) + tail
        w = jnp.where(mask, jnp.exp(log_w), 0.0)
        return jnp.einsum("bhqk,bhkd->bhqd", w.astype(v.dtype), v)

    out = lax.map(one_block, (q_blocks, jnp.arange(n_blocks)))
    return out.transpose(1, 2, 0, 3, 4).reshape(bsz, nh, seq, dh)


def stick_breaking_mixer(hn, w_qkv, w_o):
    bsz, seq, _ = hn.shape
    qkv = hn @ w_qkv
    q, k, v = jnp.split(qkv, 3, axis=-1)
    to_heads = lambda t: t.reshape(bsz, seq, SB_HEADS, SB_HEAD_DIM).transpose(0, 2, 1, 3)
    o = stick_breaking_attention(to_heads(q), to_heads(k), to_heads(v))
    o = o.transpose(0, 2, 1, 3).reshape(bsz, seq, D_MODEL)
    return o @ w_o


def squared_relu_mlp(hn, w_up, w_down):
    return jnp.square(jax.nn.relu(hn @ w_up)) @ w_down


def setup_inputs(seed: int = 0) -> dict:
    key = jax.random.key(seed)
    ks = jax.random.split(key, 32)
    f32 = jnp.float32
    nrm = lambda k, shape, fan_in: jax.random.normal(k, shape, f32) * (fan_in ** -0.5)
    gain = lambda k, shape: 1.0 + 0.05 * jax.random.normal(k, shape, f32)
    small = lambda k, shape: 0.02 * jax.random.normal(k, shape, f32)
    a_c = jax.random.uniform(ks[16], (N_EVEN, D_B), f32, 0.9, 0.999)
    a0 = a_c ** (1.0 / RG_C)
    rg_lambda = jnp.log(a0) - jnp.log1p(-a0)
    return {
        "x": jax.random.normal(ks[0], (BATCH, SEQ, D_MODEL), f32),
        "p": jax.random.normal(ks[1], (DEPTH, BATCH, SEQ, PLE_DIM), f32),
        "norm_mix_g": gain(ks[2], (DEPTH, D_MODEL)),
        "norm_mlp_g": gain(ks[3], (DEPTH, D_MODEL)),
        "norm_ple_g": gain(ks[4], (DEPTH, D_MODEL)),
        "norm_f_g": gain(ks[5], (D_MODEL,)),
        "w_in_rec": nrm(ks[6], (N_EVEN, D_MODEL, D_IN_REC), D_MODEL),
        "conv_a_w": nrm(ks[7], (N_EVEN, CONV_A_WIDTH, D_A), CONV_A_WIDTH),
        "conv_a_b": small(ks[8], (N_EVEN, D_A)),
        "ln_a_g": gain(ks[9], (N_EVEN, D_A)),
        "ln_a_b": small(ks[10], (N_EVEN, D_A)),
        "conv_b_w": nrm(ks[11], (N_EVEN, CONV_B_WIDTH, D_B), CONV_B_WIDTH),
        "conv_b_b": small(ks[12], (N_EVEN, D_B)),
        "w_rg_a": nrm(ks[13], (N_EVEN, RG_HEADS, RG_HEAD_DIM, RG_HEAD_DIM), RG_HEAD_DIM),
        "b_rg_a": small(ks[14], (N_EVEN, D_B)),
        "w_rg_x": nrm(ks[15], (N_EVEN, RG_HEADS, RG_HEAD_DIM, RG_HEAD_DIM), RG_HEAD_DIM),
        "b_rg_x": small(ks[17], (N_EVEN, D_B)),
        "rg_lambda": rg_lambda,
        "w_out_rec": nrm(ks[18], (N_EVEN, D_A + D_B, D_MODEL), D_A + D_B),
        "w_qkv": nrm(ks[19], (N_ODD, D_MODEL, 3 * D_MODEL), D_MODEL),
        "w_o_attn": nrm(ks[20], (N_ODD, D_MODEL, D_MODEL), D_MODEL),
        "w_mlp_up": nrm(ks[21], (DEPTH, D_MODEL, D_FF), D_MODEL),
        "w_mlp_down": nrm(ks[22], (DEPTH, D_FF, D_MODEL), D_FF),
        "w_ple_proj": nrm(ks[23], (DEPTH, PLE_DIM, D_MODEL), PLE_DIM),
        "w_ple_gate": nrm(ks[24], (DEPTH, D_MODEL, D_MODEL), D_MODEL),
    }


def reference(x, p, norm_mix_g, norm_mlp_g, norm_ple_g, norm_f_g, w_in_rec, conv_a_w, conv_a_b,
              ln_a_g, ln_a_b, conv_b_w, conv_b_b, w_rg_a, b_rg_a, w_rg_x, b_rg_x, rg_lambda,
              w_out_rec, w_qkv, w_o_attn, w_mlp_up, w_mlp_down, w_ple_proj, w_ple_gate):
    h = x
    for i in range(DEPTH):
        j = i // 2
        hn = rms_norm(h, norm_mix_g[i])
        if i % 2 == 0:
            mix = conv_recurrent_mixer(hn, w_in_rec[j], conv_a_w[j], conv_a_b[j], ln_a_g[j], ln_a_b[j],
                                       conv_b_w[j], conv_b_b[j], w_rg_a[j], b_rg_a[j], w_rg_x[j],
                                       b_rg_x[j], rg_lambda[j], w_out_rec[j])
        else:
            mix = stick_breaking_mixer(hn, w_qkv[j], w_o_attn[j])
        h = h + mix
        h = h + squared_relu_mlp(rms_norm(h, norm_mlp_g[i]), w_mlp_up[i], w_mlp_down[i])
        gate = jax.nn.sigmoid(rms_norm(h, norm_ple_g[i]) @ w_ple_gate[i])
        h = h + (p[i] @ w_ple_proj[i]) * gate
    return rms_norm(h, norm_f_g)
```

```python
import functools
import math

import jax
import jax.numpy as jnp
from jax import lax
from jax.experimental import pallas as pl
from jax.experimental.pallas import tpu as pltpu

F32 = jnp.float32
BF16 = jnp.bfloat16

EPS = 1e-6
RG_C = 8.0
RG_HEADS = 8
SB_HEADS = 16
CONV_A_HALO = 32
CONV_B_HALO = 8
LANES = 128
VMEM_LIMIT = 56 * 1024 * 1024


def _params(semantics):
    return pltpu.CompilerParams(dimension_semantics=semantics, vmem_limit_bytes=VMEM_LIMIT)


def _rms_normed(x, g):
    ms = jnp.mean(x * x, axis=-1, keepdims=True)
    return x * lax.rsqrt(ms + EPS) * g


def _sigmoid(x):
    return 1.0 / (1.0 + jnp.exp(-x))


def _softplus(x):
    return jnp.maximum(x, 0.0) + jnp.log(1.0 + jnp.exp(-jnp.abs(x)))


def _norm_matmul_kernel(x_ref, g_ref, w_ref, o_ref, xn_ref):
    @pl.when(pl.program_id(1) == 0)
    def _():
        xn_ref[...] = _rms_normed(x_ref[...], g_ref[...]).astype(BF16)

    o_ref[...] = jnp.dot(xn_ref[...], w_ref[...], preferred_element_type=F32).astype(o_ref.dtype)


def _norm_matmul(x, g, w, out_dtype, tm=512, tn=1024):
    m, d = x.shape
    n = w.shape[1]
    return pl.pallas_call(
        _norm_matmul_kernel,
        grid=(m // tm, n // tn),
        in_specs=[
            pl.BlockSpec((tm, d), lambda i, j: (i, 0)),
            pl.BlockSpec((1, d), lambda i, j: (0, 0)),
            pl.BlockSpec((d, tn), lambda i, j: (0, j)),
        ],
        out_specs=pl.BlockSpec((tm, tn), lambda i, j: (i, j)),
        out_shape=jax.ShapeDtypeStruct((m, n), out_dtype),
        scratch_shapes=[pltpu.VMEM((tm, d), BF16)],
        compiler_params=_params(("parallel", "arbitrary")),
        name="norm_matmul",
    )(x, g.reshape(1, d), w)


def _matmul_residual_kernel(y_ref, w_ref, h_ref, o_ref):
    o_ref[...] = h_ref[...] + jnp.dot(y_ref[...], w_ref[...], preferred_element_type=F32)


def _matmul_residual(y, w, h, tm=512):
    m, k = y.shape
    n = w.shape[1]
    return pl.pallas_call(
        _matmul_residual_kernel,
        grid=(m // tm,),
        in_specs=[
            pl.BlockSpec((tm, k), lambda i: (i, 0)),
            pl.BlockSpec((k, n), lambda i: (0, 0)),
            pl.BlockSpec((tm, n), lambda i: (i, 0)),
        ],
        out_specs=pl.BlockSpec((tm, n), lambda i: (i, 0)),
        out_shape=jax.ShapeDtypeStruct((m, n), F32),
        compiler_params=_params(("parallel",)),
        name="matmul_residual",
    )(y, w, h)


def _mlp_kernel(h_ref, g_ref, wu_ref, wd_ref, o_ref, xn_ref):
    @pl.when(pl.program_id(1) == 0)
    def _():
        h = h_ref[...]
        xn_ref[...] = _rms_normed(h, g_ref[...]).astype(BF16)
        o_ref[...] = h

    a = jnp.dot(xn_ref[...], wu_ref[...], preferred_element_type=F32)
    a = jnp.maximum(a, 0.0)
    a = (a * a).astype(BF16)
    o_ref[...] += jnp.dot(a, wd_ref[...], preferred_element_type=F32)


def _mlp(h, g, w_up, w_down, tm=512, tf=512):
    m, d = h.shape
    f = w_up.shape[1]
    return pl.pallas_call(
        _mlp_kernel,
        grid=(m // tm, f // tf),
        in_specs=[
            pl.BlockSpec((tm, d), lambda i, j: (i, 0)),
            pl.BlockSpec((1, d), lambda i, j: (0, 0)),
            pl.BlockSpec((d, tf), lambda i, j: (0, j)),
            pl.BlockSpec((tf, d), lambda i, j: (j, 0)),
        ],
        out_specs=pl.BlockSpec((tm, d), lambda i, j: (i, 0)),
        out_shape=jax.ShapeDtypeStruct((m, d), F32),
        scratch_shapes=[pltpu.VMEM((tm, d), BF16)],
        compiler_params=_params(("parallel", "arbitrary")),
        name="mlp",
    )(h, g.reshape(1, d), w_up, w_down)


def _ple_kernel(h_ref, g_ref, wg_ref, p_ref, wp_ref, gf_ref, o_ref, *, final_norm):
    h = h_ref[...]
    xn = _rms_normed(h, g_ref[...]).astype(BF16)
    gate = _sigmoid(jnp.dot(xn, wg_ref[...], preferred_element_type=F32))
    emb = jnp.dot(p_ref[...].astype(BF16), wp_ref[...], preferred_element_type=F32)
    out = h + emb * gate
    if final_norm:
        out = _rms_normed(out, gf_ref[...])
    o_ref[...] = out


def _ple(h, g, w_gate, p, w_proj, g_final, final_norm, tm=256):
    m, d = h.shape
    pd = p.shape[1]
    return pl.pallas_call(
        functools.partial(_ple_kernel, final_norm=final_norm),
        grid=(m // tm,),
        in_specs=[
            pl.BlockSpec((tm, d), lambda i: (i, 0)),
            pl.BlockSpec((1, d), lambda i: (0, 0)),
            pl.BlockSpec((d, d), lambda i: (0, 0)),
            pl.BlockSpec((tm, pd), lambda i: (i, 0)),
            pl.BlockSpec((pd, d), lambda i: (0, 0)),
            pl.BlockSpec((1, d), lambda i: (0, 0)),
        ],
        out_specs=pl.BlockSpec((tm, d), lambda i: (i, 0)),
        out_shape=jax.ShapeDtypeStruct((m, d), F32),
        compiler_params=_params(("parallel",)),
        name="ple",
    )(h, g.reshape(1, d), w_gate, p, w_proj, g_final.reshape(1, d))


def _shift_rows(x, d, fill):
    rolled = pltpu.roll(x, d, 0)
    row = lax.broadcasted_iota(jnp.int32, x.shape, 0)
    return jnp.where(row >= d, rolled, fill)


def _recmix_kernel(u_ref, caw_ref, cab_ref, lng_ref, lnb_ref, cbw_ref, cbb_ref,
                   wa_ref, ba_ref, wx_ref, bx_ref, lam_ref, o_ref,
                   glu_buf, conv_buf, xr_buf, hc_ref, *, ts, ch):
    n_chunks = ch // LANES
    ka = caw_ref.shape[0]
    kb = cbw_ref.shape[0]

    @pl.when(pl.program_id(0) == 0)
    def _():
        glu_buf[0:CONV_A_HALO, :] = jnp.zeros((CONV_A_HALO, ch), F32)
        xr_buf[0:CONV_B_HALO, :] = jnp.zeros((CONV_B_HALO, ch), F32)
        hc_ref[...] = jnp.zeros(hc_ref.shape, F32)

    glu_buf[CONV_A_HALO:CONV_A_HALO + ts, :] = u_ref[:, 0:ch] * _sigmoid(u_ref[:, ch:2 * ch])
    for c in range(n_chunks):
        cs = slice(c * LANES, (c + 1) * LANES)
        acc = jnp.broadcast_to(cab_ref[:, cs], (ts, LANES))
        for k in range(ka):
            off = CONV_A_HALO - (ka - 1) + k
            acc = acc + caw_ref[k:k + 1, cs] * glu_buf[off:off + ts, cs]
        conv_buf[:, cs] = acc
    glu_buf[0:CONV_A_HALO, :] = glu_buf[ts:ts + CONV_A_HALO, :]

    ya = conv_buf[...]
    mu = jnp.mean(ya, axis=-1, keepdims=True)
    yc = ya - mu
    var = jnp.mean(yc * yc, axis=-1, keepdims=True)
    yn = yc * lax.rsqrt(var + EPS) * lng_ref[...] + lnb_ref[...]
    o_ref[:, 0:ch] = (yn * _sigmoid(yn)).astype(o_ref.dtype)

    xr_buf[CONV_B_HALO:CONV_B_HALO + ts, :] = u_ref[:, 2 * ch:3 * ch]
    for c in range(n_chunks):
        cs = slice(c * LANES, (c + 1) * LANES)
        xc = jnp.broadcast_to(cbb_ref[:, cs], (ts, LANES))
        for k in range(kb):
            off = CONV_B_HALO - (kb - 1) + k
            xc = xc + cbw_ref[k:k + 1, cs] * xr_buf[off:off + ts, cs]
        xcb = xc.astype(BF16)
        r = _sigmoid(jnp.dot(xcb, wa_ref[c], preferred_element_type=F32) + ba_ref[:, cs])
        gi = _sigmoid(jnp.dot(xcb, wx_ref[c], preferred_element_type=F32) + bx_ref[:, cs])
        log_a = (-RG_C) * r * _softplus(-lam_ref[:, cs])
        a = jnp.exp(log_a)
        uu = jnp.sqrt(1.0 - jnp.exp(2.0 * log_a)) * (gi * xc)
        d = 1
        while d < ts:
            a_prev = _shift_rows(a, d, 1.0)
            u_prev = _shift_rows(uu, d, 0.0)
            uu = a * u_prev + uu
            a = a * a_prev
            d *= 2
        h = uu + a * hc_ref[:, cs]
        hc_ref[:, cs] = h[ts - 1:ts, :]
        gr = u_ref[:, 3 * ch + c * LANES:3 * ch + (c + 1) * LANES]
        gelu = 0.5 * gr * (1.0 + jnp.tanh(math.sqrt(2.0 / math.pi) * (gr + 0.044715 * (gr * gr * gr))))
        o_ref[:, ch + c * LANES:ch + (c + 1) * LANES] = (h * gelu).astype(o_ref.dtype)
    xr_buf[0:CONV_B_HALO, :] = xr_buf[ts:ts + CONV_B_HALO, :]


def _recmix(u, caw, cab, lng, lnb, cbw, cbb, wa, ba, wx, bx, lam, ts=256):
    s, w4 = u.shape
    ch = w4 // 4
    row = lambda v: v.reshape(1, ch)
    const2 = lambda i: (0, 0)
    const3 = lambda i: (0, 0, 0)
    return pl.pallas_call(
        functools.partial(_recmix_kernel, ts=ts, ch=ch),
        grid=(s // ts,),
        in_specs=[
            pl.BlockSpec((ts, w4), lambda i: (i, 0)),
            pl.BlockSpec(caw.shape, const2),
            pl.BlockSpec((1, ch), const2),
            pl.BlockSpec((1, ch), const2),
            pl.BlockSpec((1, ch), const2),
            pl.BlockSpec(cbw.shape, const2),
            pl.BlockSpec((1, ch), const2),
            pl.BlockSpec(wa.shape, const3),
            pl.BlockSpec((1, ch), const2),
            pl.BlockSpec(wx.shape, const3),
            pl.BlockSpec((1, ch), const2),
            pl.BlockSpec((1, ch), const2),
        ],
        out_specs=pl.BlockSpec((ts, 2 * ch), lambda i: (i, 0)),
        out_shape=jax.ShapeDtypeStruct((s, 2 * ch), BF16),
        scratch_shapes=[
            pltpu.VMEM((ts + CONV_A_HALO, ch), F32),
            pltpu.VMEM((ts, ch), F32),
            pltpu.VMEM((ts + CONV_B_HALO, ch), F32),
            pltpu.VMEM((1, ch), F32),
        ],
        compiler_params=_params(("arbitrary",)),
        name="recmix",
    )(u, caw, row(cab), row(lng), row(lnb), cbw, row(cbb), wa, row(ba), wx, row(bx), row(lam))


def _attn_kernel(q_ref, k_ref, v_ref, tri_ref, o_ref, *, blk, scale):
    qi = pl.program_id(1)
    q = q_ref[...]
    tri = tri_ref[...]
    nt = (((1,), (1,)), ((), ()))

    def block(kb, carry, acc, diagonal):
        start = pl.multiple_of(kb * blk, blk)
        k = k_ref[pl.ds(start, blk), :]
        v = v_ref[pl.ds(start, blk), :]
        z = lax.dot_general(q, k, nt, preferred_element_type=F32) * scale
        lom = -_softplus(z)
        if diagonal:
            row = lax.broadcasted_iota(jnp.int32, (blk, blk), 0)
            col = lax.broadcasted_iota(jnp.int32, (blk, blk), 1)
            mask = col < row
            lom = jnp.where(mask, lom, 0.0)
        hi = lom.astype(BF16)
        lo = (lom - hi.astype(F32)).astype(BF16)
        tail = (jnp.dot(hi, tri, preferred_element_type=F32)
                + jnp.dot(lo, tri, preferred_element_type=F32))
        w = jnp.exp(z + tail + carry)
        if diagonal:
            w = jnp.where(mask, w, 0.0)
        acc = acc + jnp.dot(w.astype(BF16), v, preferred_element_type=F32)
        carry = carry + tail[:, 0:1]
        return carry, acc

    carry0 = jnp.zeros((blk, 1), F32)
    acc0 = jnp.zeros((blk, q.shape[1]), F32)
    carry, acc = block(qi, carry0, acc0, True)

    def body(j, state):
        return block(qi - 1 - j, state[0], state[1], False)

    carry, acc = lax.fori_loop(0, qi, body, (carry, acc))
    o_ref[...] = acc.astype(o_ref.dtype)


def _attention(qkv, n_heads, blk=256):
    s, w3 = qkv.shape
    dh = w3 // (3 * n_heads)
    tri = (jnp.arange(blk)[:, None] >= jnp.arange(blk)[None, :]).astype(BF16)
    return pl.pallas_call(
        functools.partial(_attn_kernel, blk=blk, scale=1.0 / math.sqrt(dh)),
        grid=(n_heads, s // blk),
        in_specs=[
            pl.BlockSpec((blk, dh), lambda h, i: (i, h)),
            pl.BlockSpec((s, dh), lambda h, i: (0, n_heads + h)),
            pl.BlockSpec((s, dh), lambda h, i: (0, 2 * n_heads + h)),
            pl.BlockSpec((blk, blk), lambda h, i: (0, 0)),
        ],
        out_specs=pl.BlockSpec((blk, dh), lambda h, i: (i, h)),
        out_shape=jax.ShapeDtypeStruct((s, n_heads * dh), BF16),
        compiler_params=_params(("parallel", "arbitrary")),
        name="sb_attention",
    )(qkv, qkv, qkv, tri)


def kernel(x, p, norm_mix_g, norm_mlp_g, norm_ple_g, norm_f_g, w_in_rec, conv_a_w, conv_a_b, ln_a_g, ln_a_b, conv_b_w, conv_b_b, w_rg_a, b_rg_a, w_rg_x, b_rg_x, rg_lambda, w_out_rec, w_qkv, w_o_attn, w_mlp_up, w_mlp_down, w_ple_proj, w_ple_gate):
    bsz, seq, d = x.shape
    depth = p.shape[0]
    outs = []
    for b in range(bsz):
        h = x[b]
        for i in range(depth):
            j = i // 2
            if i % 2 == 0:
                u = _norm_matmul(h, norm_mix_g[i], w_in_rec[j].astype(BF16), F32)
                y = _recmix(u, conv_a_w[j], conv_a_b[j], ln_a_g[j], ln_a_b[j], conv_b_w[j], conv_b_b[j],
                            w_rg_a[j].astype(BF16), b_rg_a[j], w_rg_x[j].astype(BF16), b_rg_x[j],
                            rg_lambda[j])
                h = _matmul_residual(y, w_out_rec[j].astype(BF16), h)
            else:
                qkv = _norm_matmul(h, norm_mix_g[i], w_qkv[j].astype(BF16), BF16)
                o = _attention(qkv, SB_HEADS)
                h = _matmul_residual(o, w_o_attn[j].astype(BF16), h)
            h = _mlp(h, norm_mlp_g[i], w_mlp_up[i].astype(BF16), w_mlp_down[i].astype(BF16))
            h = _ple(h, norm_ple_g[i], w_ple_gate[i].astype(BF16), p[i, b], w_ple_proj[i].astype(BF16),
                     norm_f_g, final_norm=(i == depth - 1))
        outs.append(h)
    return jnp.stack(outs, axis=0)
```

```python
import functools
import math

import jax
import jax.numpy as jnp
from jax import lax
from jax.experimental import pallas as pl
from jax.experimental.pallas import tpu as pltpu

F32 = jnp.float32
BF16 = jnp.bfloat16

EPS = 1e-6
RG_C = 8.0
RG_HEADS = 8
SB_HEADS = 16
CONV_A_HALO = 32
CONV_B_HALO = 8
LANES = 128
VMEM_LIMIT = 56 * 1024 * 1024


def _params(semantics):
    return pltpu.CompilerParams(dimension_semantics=semantics, vmem_limit_bytes=VMEM_LIMIT)


def _rms_normed(x, g):
    ms = jnp.mean(x * x, axis=-1, keepdims=True)
    return x * lax.rsqrt(ms + EPS) * g


def _sigmoid(x):
    return 1.0 / (1.0 + jnp.exp(-x))


def _softplus(x):
    return jnp.maximum(x, 0.0) + jnp.log(1.0 + jnp.exp(-jnp.abs(x)))


def _norm_matmul_kernel(x_ref, g_ref, w_ref, *rest, scaled):
    cs_ref = rest[0] if scaled else None
    o_ref, xn_ref = rest[-2:]

    @pl.when(pl.program_id(1) == 0)
    def _():
        xn_ref[...] = _rms_normed(x_ref[...], g_ref[...]).astype(BF16)

    acc = jnp.dot(xn_ref[...], w_ref[...], preferred_element_type=F32)
    if scaled:
        acc = acc * cs_ref[...]
    o_ref[...] = acc.astype(o_ref.dtype)


def _norm_matmul(x, g, w, out_dtype, col_scale=None, tm=512, tn=1024):
    m, d = x.shape
    n = w.shape[1]
    scaled = col_scale is not None
    in_specs = [
        pl.BlockSpec((tm, d), lambda i, j: (i, 0)),
        pl.BlockSpec((1, d), lambda i, j: (0, 0)),
        pl.BlockSpec((d, tn), lambda i, j: (0, j)),
    ]
    args = [x, g.reshape(1, d), w]
    if scaled:
        in_specs.append(pl.BlockSpec((1, tn), lambda i, j: (0, j)))
        args.append(col_scale.reshape(1, n))
    return pl.pallas_call(
        functools.partial(_norm_matmul_kernel, scaled=scaled),
        grid=(m // tm, n // tn),
        in_specs=in_specs,
        out_specs=pl.BlockSpec((tm, tn), lambda i, j: (i, j)),
        out_shape=jax.ShapeDtypeStruct((m, n), out_dtype),
        scratch_shapes=[pltpu.VMEM((tm, d), BF16)],
        compiler_params=_params(("parallel", "arbitrary")),
        name="norm_matmul",
    )(*args)


def _matmul_residual_kernel(y_ref, w_ref, h_ref, o_ref):
    o_ref[...] = h_ref[...] + jnp.dot(y_ref[...], w_ref[...], preferred_element_type=F32)


def _matmul_residual(y, w, h, tm=512):
    m, k = y.shape
    n = w.shape[1]
    return pl.pallas_call(
        _matmul_residual_kernel,
        grid=(m // tm,),
        in_specs=[
            pl.BlockSpec((tm, k), lambda i: (i, 0)),
            pl.BlockSpec((k, n), lambda i: (0, 0)),
            pl.BlockSpec((tm, n), lambda i: (i, 0)),
        ],
        out_specs=pl.BlockSpec((tm, n), lambda i: (i, 0)),
        out_shape=jax.ShapeDtypeStruct((m, n), F32),
        compiler_params=_params(("parallel",)),
        name="matmul_residual",
    )(y, w, h)


def _mlp_kernel(h_ref, g_ref, wu_ref, wd_ref, o_ref, xn_ref):
    @pl.when(pl.program_id(1) == 0)
    def _():
        h = h_ref[...]
        xn_ref[...] = _rms_normed(h, g_ref[...]).astype(BF16)
        o_ref[...] = h

    a = jnp.dot(xn_ref[...], wu_ref[...], preferred_element_type=F32)
    a = jnp.maximum(a, 0.0)
    a = (a * a).astype(BF16)
    o_ref[...] += jnp.dot(a, wd_ref[...], preferred_element_type=F32)


def _mlp(h, g, w_up, w_down, tm=512, tf=512):
    m, d = h.shape
    f = w_up.shape[1]
    return pl.pallas_call(
        _mlp_kernel,
        grid=(m // tm, f // tf),
        in_specs=[
            pl.BlockSpec((tm, d), lambda i, j: (i, 0)),
            pl.BlockSpec((1, d), lambda i, j: (0, 0)),
            pl.BlockSpec((d, tf), lambda i, j: (0, j)),
            pl.BlockSpec((tf, d), lambda i, j: (j, 0)),
        ],
        out_specs=pl.BlockSpec((tm, d), lambda i, j: (i, 0)),
        out_shape=jax.ShapeDtypeStruct((m, d), F32),
        scratch_shapes=[pltpu.VMEM((tm, d), BF16)],
        compiler_params=_params(("parallel", "arbitrary")),
        name="mlp",
    )(h, g.reshape(1, d), w_up, w_down)


def _ple_kernel(h_ref, g_ref, wg_ref, p_ref, wp_ref, gf_ref, o_ref, *, final_norm):
    h = h_ref[...]
    xn = _rms_normed(h, g_ref[...]).astype(BF16)
    gate = _sigmoid(jnp.dot(xn, wg_ref[...], preferred_element_type=F32))
    emb = jnp.dot(p_ref[...].astype(BF16), wp_ref[...], preferred_element_type=F32)
    out = h + emb * gate
    if final_norm:
        out = _rms_normed(out, gf_ref[...])
    o_ref[...] = out


def _ple(h, g, w_gate, p, w_proj, g_final, final_norm, tm=256):
    m, d = h.shape
    pd = p.shape[1]
    return pl.pallas_call(
        functools.partial(_ple_kernel, final_norm=final_norm),
        grid=(m // tm,),
        in_specs=[
            pl.BlockSpec((tm, d), lambda i: (i, 0)),
            pl.BlockSpec((1, d), lambda i: (0, 0)),
            pl.BlockSpec((d, d), lambda i: (0, 0)),
            pl.BlockSpec((tm, pd), lambda i: (i, 0)),
            pl.BlockSpec((pd, d), lambda i: (0, 0)),
            pl.BlockSpec((1, d), lambda i: (0, 0)),
        ],
        out_specs=pl.BlockSpec((tm, d), lambda i: (i, 0)),
        out_shape=jax.ShapeDtypeStruct((m, d), F32),
        compiler_params=_params(("parallel",)),
        name="ple",
    )(h, g.reshape(1, d), w_gate, p, w_proj, g_final.reshape(1, d))


def _shift_rows(x, d, fill):
    rolled = pltpu.roll(x, d, 0)
    row = lax.broadcasted_iota(jnp.int32, x.shape, 0)
    return jnp.where(row >= d, rolled, fill)


def _recmix_kernel(u_ref, caw_ref, cab_ref, lng_ref, lnb_ref, cbw_ref, cbb_ref,
                   wa_ref, ba_ref, wx_ref, bx_ref, lam_ref, o_ref,
                   glu_buf, conv_buf, xr_buf, hc_ref, *, ts, ch):
    n_chunks = ch // LANES
    ka = caw_ref.shape[0]
    kb = cbw_ref.shape[0]

    @pl.when(pl.program_id(0) == 0)
    def _():
        glu_buf[0:CONV_A_HALO, :] = jnp.zeros((CONV_A_HALO, ch), F32)
        xr_buf[0:CONV_B_HALO, :] = jnp.zeros((CONV_B_HALO, ch), F32)
        hc_ref[...] = jnp.zeros(hc_ref.shape, F32)

    glu_buf[CONV_A_HALO:CONV_A_HALO + ts, :] = u_ref[:, 0:ch] * _sigmoid(u_ref[:, ch:2 * ch])
    for c in range(n_chunks):
        cs = slice(c * LANES, (c + 1) * LANES)
        acc = jnp.broadcast_to(cab_ref[:, cs], (ts, LANES))
        for k in range(ka):
            off = CONV_A_HALO - (ka - 1) + k
            acc = acc + caw_ref[k:k + 1, cs] * glu_buf[off:off + ts, cs]
        conv_buf[:, cs] = acc
    glu_buf[0:CONV_A_HALO, :] = glu_buf[ts:ts + CONV_A_HALO, :]

    ya = conv_buf[...]
    mu = jnp.mean(ya, axis=-1, keepdims=True)
    yc = ya - mu
    var = jnp.mean(yc * yc, axis=-1, keepdims=True)
    yn = yc * lax.rsqrt(var + EPS) * lng_ref[...] + lnb_ref[...]
    o_ref[:, 0:ch] = (yn * _sigmoid(yn)).astype(o_ref.dtype)

    xr_buf[CONV_B_HALO:CONV_B_HALO + ts, :] = u_ref[:, 2 * ch:3 * ch]
    for c in range(n_chunks):
        cs = slice(c * LANES, (c + 1) * LANES)
        xc = jnp.broadcast_to(cbb_ref[:, cs], (ts, LANES))
        for k in range(kb):
            off = CONV_B_HALO - (kb - 1) + k
            xc = xc + cbw_ref[k:k + 1, cs] * xr_buf[off:off + ts, cs]
        xcb = xc.astype(BF16)
        r = _sigmoid(jnp.dot(xcb, wa_ref[c], preferred_element_type=F32) + ba_ref[:, cs])
        gi = _sigmoid(jnp.dot(xcb, wx_ref[c], preferred_element_type=F32) + bx_ref[:, cs])
        log_a = (-RG_C) * r * _softplus(-lam_ref[:, cs])
        a = jnp.exp(log_a)
        uu = jnp.sqrt(1.0 - jnp.exp(2.0 * log_a)) * (gi * xc)
        d = 1
        while d < ts:
            a_prev = _shift_rows(a, d, 1.0)
            u_prev = _shift_rows(uu, d, 0.0)
            uu = a * u_prev + uu
            a = a * a_prev
            d *= 2
        h = uu + a * hc_ref[:, cs]
        hc_ref[:, cs] = h[ts - 1:ts, :]
        gr = u_ref[:, 3 * ch + c * LANES:3 * ch + (c + 1) * LANES]
        gelu = 0.5 * gr * (1.0 + jnp.tanh(math.sqrt(2.0 / math.pi) * (gr + 0.044715 * (gr * gr * gr))))
        o_ref[:, ch + c * LANES:ch + (c + 1) * LANES] = (h * gelu).astype(o_ref.dtype)
    xr_buf[0:CONV_B_HALO, :] = xr_buf[ts:ts + CONV_B_HALO, :]


def _recmix(u, caw, cab, lng, lnb, cbw, cbb, wa, ba, wx, bx, lam, ts=256):
    s, w4 = u.shape
    ch = w4 // 4
    row = lambda v: v.reshape(1, ch)
    const2 = lambda i: (0, 0)
    const3 = lambda i: (0, 0, 0)
    return pl.pallas_call(
        functools.partial(_recmix_kernel, ts=ts, ch=ch),
        grid=(s // ts,),
        in_specs=[
            pl.BlockSpec((ts, w4), lambda i: (i, 0)),
            pl.BlockSpec(caw.shape, const2),
            pl.BlockSpec((1, ch), const2),
            pl.BlockSpec((1, ch), const2),
            pl.BlockSpec((1, ch), const2),
            pl.BlockSpec(cbw.shape, const2),
            pl.BlockSpec((1, ch), const2),
            pl.BlockSpec(wa.shape, const3),
            pl.BlockSpec((1, ch), const2),
            pl.BlockSpec(wx.shape, const3),
            pl.BlockSpec((1, ch), const2),
            pl.BlockSpec((1, ch), const2),
        ],
        out_specs=pl.BlockSpec((ts, 2 * ch), lambda i: (i, 0)),
        out_shape=jax.ShapeDtypeStruct((s, 2 * ch), BF16),
        scratch_shapes=[
            pltpu.VMEM((ts + CONV_A_HALO, ch), F32),
            pltpu.VMEM((ts, ch), F32),
            pltpu.VMEM((ts + CONV_B_HALO, ch), F32),
            pltpu.VMEM((1, ch), F32),
        ],
        compiler_params=_params(("arbitrary",)),
        name="recmix",
    )(u, caw, row(cab), row(lng), row(lnb), cbw, row(cbb), wa, row(ba), wx, row(bx), row(lam))


ATTN_HEADS_PER_STEP = 2
ATTN_KEY_BLOCKS_PER_TRIP = 2
LOG2E = 1.4426950408889634
SIGN_BIT = -2147483648


def _attn_kernel(q_ref, k_ref, v_ref, ntri_ref, o_ref, *, bq, bk, dh):
    qi = pl.program_id(1)
    ntri = ntri_ref[...]
    nt = (((1,), (1,)), ((), ()))
    heads = range(ATTN_HEADS_PER_STEP)
    step = ATTN_KEY_BLOCKS_PER_TRIP
    qs = [q_ref[:, hh * dh:(hh + 1) * dh] for hh in heads]
    todo = [(hh, n) for hh in heads for n in range(step)]

    def block_ref(ref, hh, kb):
        return ref[pl.ds(pl.multiple_of(kb * bk, bk), bk), hh * dh:(hh + 1) * dh]

    def scores(kbs):
        return [lax.dot_general(qs[hh], block_ref(k_ref, hh, kbs[n]), nt, preferred_element_type=F32)
                for hh, n in todo]

    def logits(zs, masks):
        xs, rowsums = [], []
        for c, (hh, n) in enumerate(todo):
            z = zs[c]
            t = lax.bitcast_convert_type(lax.bitcast_convert_type(z, jnp.int32) | SIGN_BIT, F32)
            sp = jnp.maximum(z, 0.0) + jnp.log(1.0 + jnp.exp2(t)) * LOG2E
            if masks is not None:
                sp = jnp.where(masks[n], sp, 0.0)
            hi = sp.astype(BF16)
            lo = (sp - hi.astype(F32)).astype(BF16)
            tail = jnp.dot(jnp.concatenate([hi, lo], axis=1), ntri, preferred_element_type=F32)
            xs.append(z + tail)
            rowsums.append(tail[:, 0:1])
        return xs, rowsums

    def weighted_values(kbs, xs, masks):
        pvs = []
        for c, (hh, n) in enumerate(todo):
            w = jnp.exp2(xs[c])
            if masks is not None:
                w = jnp.where(masks[n], w, 0.0)
            pvs.append(jnp.dot(w.astype(BF16), block_ref(v_ref, hh, kbs[n]), preferred_element_type=F32))
        return pvs

    def accumulate(state, pvs, rowsums):
        new = []
        for hh in heads:
            carry, acc = state[hh]
            for c, (h2, n) in enumerate(todo):
                if h2 == hh:
                    acc = acc + jnp.exp2(carry) * pvs[c]
                    carry = carry + rowsums[c]
            new.append((carry, acc))
        return tuple(new)

    def sweep(state, kbs, masks=None):
        xs, rowsums = logits(scores(kbs), masks)
        return accumulate(state, weighted_values(kbs, xs, masks), rowsums)

    row = lax.broadcasted_iota(jnp.int32, (bq, bk), 0)
    col = lax.broadcasted_iota(jnp.int32, (bq, bk), 1)
    offsets = list(reversed(range(step)))
    zero = (jnp.zeros((bq, 1), F32), jnp.zeros((bq, dh), F32))
    state = sweep(tuple(zero for _ in heads), [step * qi + n for n in offsets],
                  [col + n * bk < row for n in offsets])

    def older(j, state):
        newest = step * (qi - j) - 1
        return sweep(state, [newest - n for n in range(step)])

    state = lax.fori_loop(0, qi, older, state)
    for hh in heads:
        o_ref[:, hh * dh:(hh + 1) * dh] = state[hh][1].astype(o_ref.dtype)


def _attention(qkv, n_heads, bq=512, bk=256):
    s, w3 = qkv.shape
    dh = w3 // (3 * n_heads)
    hps = ATTN_HEADS_PER_STEP
    groups = n_heads // hps
    assert bq == ATTN_KEY_BLOCKS_PER_TRIP * bk
    tri = (jnp.arange(bk)[:, None] >= jnp.arange(bk)[None, :])
    ntri = -jnp.concatenate([tri, tri], axis=0).astype(BF16)
    return pl.pallas_call(
        functools.partial(_attn_kernel, bq=bq, bk=bk, dh=dh),
        grid=(groups, s // bq),
        in_specs=[
            pl.BlockSpec((bq, hps * dh), lambda g, i: (i, g)),
            pl.BlockSpec((s, hps * dh), lambda g, i: (0, groups + g)),
            pl.BlockSpec((s, hps * dh), lambda g, i: (0, 2 * groups + g)),
            pl.BlockSpec((2 * bk, bk), lambda g, i: (0, 0)),
        ],
        out_specs=pl.BlockSpec((bq, hps * dh), lambda g, i: (i, g)),
        out_shape=jax.ShapeDtypeStruct((s, n_heads * dh), BF16),
        compiler_params=_params(("parallel", "arbitrary")),
        name="sb_attention",
    )(qkv, qkv, qkv, ntri)


def kernel(x, p, norm_mix_g, norm_mlp_g, norm_ple_g, norm_f_g, w_in_rec, conv_a_w, conv_a_b, ln_a_g, ln_a_b, conv_b_w, conv_b_b, w_rg_a, b_rg_a, w_rg_x, b_rg_x, rg_lambda, w_out_rec, w_qkv, w_o_attn, w_mlp_up, w_mlp_down, w_ple_proj, w_ple_gate):
    bsz, seq, d = x.shape
    depth = p.shape[0]
    outs = []
    for b in range(bsz):
        h = x[b]
        for i in range(depth):
            j = i // 2
            if i % 2 == 0:
                u = _norm_matmul(h, norm_mix_g[i], w_in_rec[j].astype(BF16), F32)
                y = _recmix(u, conv_a_w[j], conv_a_b[j], ln_a_g[j], ln_a_b[j], conv_b_w[j], conv_b_b[j],
                            w_rg_a[j].astype(BF16), b_rg_a[j], w_rg_x[j].astype(BF16), b_rg_x[j],
                            rg_lambda[j])
                h = _matmul_residual(y, w_out_rec[j].astype(BF16), h)
            else:
                q_scale = LOG2E / math.sqrt(d // SB_HEADS)
                col_scale = jnp.concatenate([jnp.full((d,), q_scale, F32), jnp.ones((2 * d,), F32)])
                qkv = _norm_matmul(h, norm_mix_g[i], w_qkv[j].astype(BF16), BF16, col_scale=col_scale)
                o = _attention(qkv, SB_HEADS)
                h = _matmul_residual(o, w_o_attn[j].astype(BF16), h)
            h = _mlp(h, norm_mlp_g[i], w_mlp_up[i].astype(BF16), w_mlp_down[i].astype(BF16))
            h = _ple(h, norm_ple_g[i], w_ple_gate[i].astype(BF16), p[i, b], w_ple_proj[i].astype(BF16),
                     norm_f_g, final_norm=(i == depth - 1))
        outs.append(h)
    return jnp.stack(outs, axis=0)
```

```python
import functools
import math

import jax
import jax.numpy as jnp
from jax import lax
from jax.experimental import pallas as pl
from jax.experimental.pallas import tpu as pltpu

F32 = jnp.float32
BF16 = jnp.bfloat16

EPS = 1e-6
RG_C = 8.0
RG_HEADS = 8
SB_HEADS = 16
CONV_A_HALO = 32
CONV_B_HALO = 8
LANES = 128
SUBLANES = 8
VMEM_LIMIT = 56 * 1024 * 1024


def _params(semantics):
    return pltpu.CompilerParams(dimension_semantics=semantics, vmem_limit_bytes=VMEM_LIMIT)


def _rms_normed(x, g):
    ms = jnp.mean(x * x, axis=-1, keepdims=True)
    return x * lax.rsqrt(ms + EPS) * g


def _sigmoid(x):
    return 1.0 / (1.0 + jnp.exp(-x))


def _softplus(x):
    return jnp.maximum(x, 0.0) + jnp.log(1.0 + jnp.exp(-jnp.abs(x)))


def _norm_matmul_kernel(x_ref, g_ref, w_ref, *rest, scaled):
    cs_ref = rest[0] if scaled else None
    o_ref, xn_ref = rest[-2:]

    @pl.when(pl.program_id(1) == 0)
    def _():
        xn_ref[...] = _rms_normed(x_ref[...], g_ref[...]).astype(BF16)

    acc = jnp.dot(xn_ref[...], w_ref[...], preferred_element_type=F32)
    if scaled:
        acc = acc * cs_ref[...]
    o_ref[...] = acc.astype(o_ref.dtype)


def _norm_matmul(x, g, w, layer, out_dtype, col_scale=None, tm=512, tn=1024):
    m, d = x.shape
    n = w.shape[2]
    scaled = col_scale is not None
    in_specs = [
        pl.BlockSpec((tm, d), lambda i, j: (i, 0)),
        pl.BlockSpec((1, d), lambda i, j: (0, 0)),
        pl.BlockSpec((None, d, tn), lambda i, j: (layer, 0, j)),
    ]
    args = [x, g.reshape(1, d), w]
    if scaled:
        in_specs.append(pl.BlockSpec((1, tn), lambda i, j: (0, j)))
        args.append(col_scale.reshape(1, n))
    return pl.pallas_call(
        functools.partial(_norm_matmul_kernel, scaled=scaled),
        grid=(m // tm, n // tn),
        in_specs=in_specs,
        out_specs=pl.BlockSpec((tm, tn), lambda i, j: (i, j)),
        out_shape=jax.ShapeDtypeStruct((m, n), out_dtype),
        scratch_shapes=[pltpu.VMEM((tm, d), BF16)],
        compiler_params=_params(("parallel", "arbitrary")),
        name="norm_matmul",
    )(*args)


def _matmul_residual_kernel(y_ref, w_ref, h_ref, o_ref):
    o_ref[...] = h_ref[...] + jnp.dot(y_ref[...], w_ref[...], preferred_element_type=F32)


def _matmul_residual(y, w, layer, h, tm=512):
    m, k = y.shape
    n = w.shape[2]
    return pl.pallas_call(
        _matmul_residual_kernel,
        grid=(m // tm,),
        in_specs=[
            pl.BlockSpec((tm, k), lambda i: (i, 0)),
            pl.BlockSpec((None, k, n), lambda i: (layer, 0, 0)),
            pl.BlockSpec((tm, n), lambda i: (i, 0)),
        ],
        out_specs=pl.BlockSpec((tm, n), lambda i: (i, 0)),
        out_shape=jax.ShapeDtypeStruct((m, n), F32),
        compiler_params=_params(("parallel",)),
        name="matmul_residual",
    )(y, w, h)


def _mlp_kernel(h_ref, g_ref, wu_ref, wd_ref, o_ref, xn_ref):
    @pl.when(pl.program_id(1) == 0)
    def _():
        h = h_ref[...]
        xn_ref[...] = _rms_normed(h, g_ref[...]).astype(BF16)
        o_ref[...] = h

    a = jnp.dot(xn_ref[...], wu_ref[...], preferred_element_type=F32)
    a = jnp.maximum(a, 0.0)
    a = (a * a).astype(BF16)
    o_ref[...] += jnp.dot(a, wd_ref[...], preferred_element_type=F32)


def _mlp(h, g, w_up, w_down, layer, tm=512, tf=512):
    m, d = h.shape
    f = w_up.shape[2]
    return pl.pallas_call(
        _mlp_kernel,
        grid=(m // tm, f // tf),
        in_specs=[
            pl.BlockSpec((tm, d), lambda i, j: (i, 0)),
            pl.BlockSpec((1, d), lambda i, j: (0, 0)),
            pl.BlockSpec((None, d, tf), lambda i, j: (layer, 0, j)),
            pl.BlockSpec((None, tf, d), lambda i, j: (layer, j, 0)),
        ],
        out_specs=pl.BlockSpec((tm, d), lambda i, j: (i, 0)),
        out_shape=jax.ShapeDtypeStruct((m, d), F32),
        scratch_shapes=[pltpu.VMEM((tm, d), BF16)],
        compiler_params=_params(("parallel", "arbitrary")),
        name="mlp",
    )(h, g.reshape(1, d), w_up, w_down)


def _ple_kernel(h_ref, g_ref, wg_ref, p_ref, wp_ref, gf_ref, o_ref, *, final_norm):
    h = h_ref[...]
    xn = _rms_normed(h, g_ref[...]).astype(BF16)
    gate = _sigmoid(jnp.dot(xn, wg_ref[...], preferred_element_type=F32))
    emb = jnp.dot(p_ref[...].astype(BF16), wp_ref[...], preferred_element_type=F32)
    out = h + emb * gate
    if final_norm:
        out = _rms_normed(out, gf_ref[...])
    o_ref[...] = out


def _ple(h, g, w_gate, p, w_proj, layer, batch, g_final, final_norm, tm=256):
    m, d = h.shape
    pd = p.shape[3]
    return pl.pallas_call(
        functools.partial(_ple_kernel, final_norm=final_norm),
        grid=(m // tm,),
        in_specs=[
            pl.BlockSpec((tm, d), lambda i: (i, 0)),
            pl.BlockSpec((1, d), lambda i: (0, 0)),
            pl.BlockSpec((None, d, d), lambda i: (layer, 0, 0)),
            pl.BlockSpec((None, None, tm, pd), lambda i: (layer, batch, i, 0)),
            pl.BlockSpec((None, pd, d), lambda i: (layer, 0, 0)),
            pl.BlockSpec((1, d), lambda i: (0, 0)),
        ],
        out_specs=pl.BlockSpec((tm, d), lambda i: (i, 0)),
        out_shape=jax.ShapeDtypeStruct((m, d), F32),
        compiler_params=_params(("parallel",)),
        name="ple",
    )(h, g.reshape(1, d), w_gate, p, w_proj, g_final.reshape(1, d))


def _shift_rows(x, d, fill):
    rolled = pltpu.roll(x, d, 0)
    row = lax.broadcasted_iota(jnp.int32, x.shape, 0)
    return jnp.where(row >= d, rolled, fill)


def _recmix_kernel(u_ref, caw_ref, cab_ref, lng_ref, lnb_ref, cbw_ref, cbb_ref,
                   wa_ref, ba_ref, wx_ref, bx_ref, lam_ref, o_ref,
                   glu_buf, conv_buf, xr_buf, hc_ref, *, ts, ch):
    n_chunks = ch // LANES
    ka = caw_ref.shape[0]
    kb = cbw_ref.shape[0]

    @pl.when(pl.program_id(0) == 0)
    def _():
        glu_buf[0:CONV_A_HALO, :] = jnp.zeros((CONV_A_HALO, ch), F32)
        xr_buf[0:CONV_B_HALO, :] = jnp.zeros((CONV_B_HALO, ch), F32)
        hc_ref[...] = jnp.zeros(hc_ref.shape, F32)

    glu_buf[CONV_A_HALO:CONV_A_HALO + ts, :] = u_ref[:, 0:ch] * _sigmoid(u_ref[:, ch:2 * ch])
    offs = [CONV_A_HALO - (ka - 1) + k for k in range(ka)]
    rows = ts + CONV_A_HALO
    for c in range(n_chunks):
        cs = slice(c * LANES, (c + 1) * LANES)
        hist = glu_buf[:, cs]
        acc = jnp.broadcast_to(cab_ref[:, cs], (ts, LANES))
        for r in range(SUBLANES):
            taps = [k for k in range(ka) if offs[k] % SUBLANES == r]
            if not taps:
                continue
            shifted = hist if r == 0 else pltpu.roll(hist, rows - r, 0)
            for k in taps:
                acc = acc + caw_ref[k:k + 1, cs] * shifted[offs[k] - r:offs[k] - r + ts, :]
        conv_buf[:, cs] = acc
    glu_buf[0:CONV_A_HALO, :] = glu_buf[ts:ts + CONV_A_HALO, :]

    ya = conv_buf[...]
    mu = jnp.mean(ya, axis=-1, keepdims=True)
    yc = ya - mu
    var = jnp.mean(yc * yc, axis=-1, keepdims=True)
    yn = yc * lax.rsqrt(var + EPS) * lng_ref[...] + lnb_ref[...]
    o_ref[:, 0:ch] = (yn * _sigmoid(yn)).astype(o_ref.dtype)

    xr_buf[CONV_B_HALO:CONV_B_HALO + ts, :] = u_ref[:, 2 * ch:3 * ch]
    for c in range(n_chunks):
        cs = slice(c * LANES, (c + 1) * LANES)
        xc = jnp.broadcast_to(cbb_ref[:, cs], (ts, LANES))
        for k in range(kb):
            off = CONV_B_HALO - (kb - 1) + k
            xc = xc + cbw_ref[k:k + 1, cs] * xr_buf[off:off + ts, cs]
        xcb = xc.astype(BF16)
        r = _sigmoid(jnp.dot(xcb, wa_ref[c], preferred_element_type=F32) + ba_ref[:, cs])
        gi = _sigmoid(jnp.dot(xcb, wx_ref[c], preferred_element_type=F32) + bx_ref[:, cs])
        log_a = (-RG_C) * r * _softplus(-lam_ref[:, cs])
        a = jnp.exp(log_a)
        uu = jnp.sqrt(1.0 - jnp.exp(2.0 * log_a)) * (gi * xc)
        d = 1
        while d < ts:
            a_prev = _shift_rows(a, d, 1.0)
            u_prev = _shift_rows(uu, d, 0.0)
            uu = a * u_prev + uu
            a = a * a_prev
            d *= 2
        h = uu + a * hc_ref[:, cs]
        hc_ref[:, cs] = h[ts - 1:ts, :]
        gr = u_ref[:, 3 * ch + c * LANES:3 * ch + (c + 1) * LANES]
        gelu = 0.5 * gr * (1.0 + jnp.tanh(math.sqrt(2.0 / math.pi) * (gr + 0.044715 * (gr * gr * gr))))
        o_ref[:, ch + c * LANES:ch + (c + 1) * LANES] = (h * gelu).astype(o_ref.dtype)
    xr_buf[0:CONV_B_HALO, :] = xr_buf[ts:ts + CONV_B_HALO, :]


def _recmix(u, caw, cab, lng, lnb, cbw, cbb, wa, ba, wx, bx, lam, layer, ts=256):
    s, w4 = u.shape
    ch = w4 // 4
    row = lambda v: v.reshape(1, ch)
    const2 = lambda i: (0, 0)
    gate_spec = pl.BlockSpec((None,) + wa.shape[1:], lambda i: (layer, 0, 0, 0))
    return pl.pallas_call(
        functools.partial(_recmix_kernel, ts=ts, ch=ch),
        grid=(s // ts,),
        in_specs=[
            pl.BlockSpec((ts, w4), lambda i: (i, 0)),
            pl.BlockSpec(caw.shape, const2),
            pl.BlockSpec((1, ch), const2),
            pl.BlockSpec((1, ch), const2),
            pl.BlockSpec((1, ch), const2),
            pl.BlockSpec(cbw.shape, const2),
            pl.BlockSpec((1, ch), const2),
            gate_spec,
            pl.BlockSpec((1, ch), const2),
            gate_spec,
            pl.BlockSpec((1, ch), const2),
            pl.BlockSpec((1, ch), const2),
        ],
        out_specs=pl.BlockSpec((ts, 2 * ch), lambda i: (i, 0)),
        out_shape=jax.ShapeDtypeStruct((s, 2 * ch), BF16),
        scratch_shapes=[
            pltpu.VMEM((ts + CONV_A_HALO, ch), F32),
            pltpu.VMEM((ts, ch), F32),
            pltpu.VMEM((ts + CONV_B_HALO, ch), F32),
            pltpu.VMEM((1, ch), F32),
        ],
        compiler_params=_params(("arbitrary",)),
        name="recmix",
    )(u, caw, row(cab), row(lng), row(lnb), cbw, row(cbb), wa, row(ba), wx, row(bx), row(lam))


ATTN_HEADS_PER_STEP = 4
ATTN_KEY_BLOCKS_PER_TRIP = 2
LOG2E = 1.4426950408889634
SIGN_BIT = -2147483648


def _attn_kernel(q_ref, k_ref, v_ref, ntri_ref, o_ref, *, bq, bk, dh):
    qi = pl.program_id(1)
    ntri = ntri_ref[...]
    nt = (((1,), (1,)), ((), ()))
    heads = range(ATTN_HEADS_PER_STEP)
    step = ATTN_KEY_BLOCKS_PER_TRIP
    qs = [q_ref[:, hh * dh:(hh + 1) * dh] for hh in heads]
    todo = [(hh, n) for hh in heads for n in range(step)]

    def block_ref(ref, hh, kb):
        return ref[pl.ds(pl.multiple_of(kb * bk, bk), bk), hh * dh:(hh + 1) * dh]

    def scores(kbs):
        return [lax.dot_general(qs[hh], block_ref(k_ref, hh, kbs[n]), nt, preferred_element_type=F32)
                for hh, n in todo]

    def logits(zs, masks):
        xs, rowsums = [], []
        for c, (hh, n) in enumerate(todo):
            z = zs[c]
            t = lax.bitcast_convert_type(lax.bitcast_convert_type(z, jnp.int32) | SIGN_BIT, F32)
            sp = jnp.maximum(z, 0.0) + jnp.log(1.0 + jnp.exp2(t)) * LOG2E
            if masks is not None:
                sp = jnp.where(masks[n], sp, 0.0)
            tail = jnp.dot(sp.astype(BF16), ntri, preferred_element_type=F32)
            xs.append(z + tail)
            rowsums.append(tail[:, 0:1])
        return xs, rowsums

    def weighted_values(kbs, xs, masks):
        pvs = []
        for c, (hh, n) in enumerate(todo):
            w = jnp.exp2(xs[c])
            if masks is not None:
                w = jnp.where(masks[n], w, 0.0)
            pvs.append(jnp.dot(w.astype(BF16), block_ref(v_ref, hh, kbs[n]), preferred_element_type=F32))
        return pvs

    def accumulate(state, pvs, rowsums):
        new = []
        for hh in heads:
            carry, acc = state[hh]
            for c, (h2, n) in enumerate(todo):
                if h2 == hh:
                    acc = acc + jnp.exp2(carry) * pvs[c]
                    carry = carry + rowsums[c]
            new.append((carry, acc))
        return tuple(new)

    def sweep(state, kbs, masks=None):
        xs, rowsums = logits(scores(kbs), masks)
        return accumulate(state, weighted_values(kbs, xs, masks), rowsums)

    row = lax.broadcasted_iota(jnp.int32, (bq, bk), 0)
    col = lax.broadcasted_iota(jnp.int32, (bq, bk), 1)
    offsets = list(reversed(range(step)))
    zero = (jnp.zeros((bq, 1), F32), jnp.zeros((bq, dh), F32))
    state = sweep(tuple(zero for _ in heads), [step * qi + n for n in offsets],
                  [col + n * bk < row for n in offsets])

    def older(j, state):
        newest = step * (qi - j) - 1
        return sweep(state, [newest - n for n in range(step)])

    state = lax.fori_loop(0, qi, older, state)
    for hh in heads:
        o_ref[:, hh * dh:(hh + 1) * dh] = state[hh][1].astype(o_ref.dtype)


def _attention(qkv, n_heads, bq=512, bk=256):
    s, w3 = qkv.shape
    dh = w3 // (3 * n_heads)
    hps = ATTN_HEADS_PER_STEP
    groups = n_heads // hps
    assert bq == ATTN_KEY_BLOCKS_PER_TRIP * bk
    tri = (jnp.arange(bk)[:, None] >= jnp.arange(bk)[None, :])
    ntri = -tri.astype(BF16)
    return pl.pallas_call(
        functools.partial(_attn_kernel, bq=bq, bk=bk, dh=dh),
        grid=(groups, s // bq),
        in_specs=[
            pl.BlockSpec((bq, hps * dh), lambda g, i: (i, g)),
            pl.BlockSpec((s, hps * dh), lambda g, i: (0, groups + g)),
            pl.BlockSpec((s, hps * dh), lambda g, i: (0, 2 * groups + g)),
            pl.BlockSpec((bk, bk), lambda g, i: (0, 0)),
        ],
        out_specs=pl.BlockSpec((bq, hps * dh), lambda g, i: (i, g)),
        out_shape=jax.ShapeDtypeStruct((s, n_heads * dh), BF16),
        compiler_params=_params(("parallel", "arbitrary")),
        name="sb_attention",
    )(qkv, qkv, qkv, ntri)


def kernel(x, p, norm_mix_g, norm_mlp_g, norm_ple_g, norm_f_g, w_in_rec, conv_a_w, conv_a_b, ln_a_g, ln_a_b, conv_b_w, conv_b_b, w_rg_a, b_rg_a, w_rg_x, b_rg_x, rg_lambda, w_out_rec, w_qkv, w_o_attn, w_mlp_up, w_mlp_down, w_ple_proj, w_ple_gate):
    bsz, seq, d = x.shape
    depth = p.shape[0]
    outs = []
    w_in_rec, w_out_rec, w_qkv, w_o_attn, w_mlp_up, w_mlp_down, w_ple_proj, w_ple_gate, w_rg_a, w_rg_x = (
        w.astype(BF16) for w in (w_in_rec, w_out_rec, w_qkv, w_o_attn, w_mlp_up, w_mlp_down,
                                 w_ple_proj, w_ple_gate, w_rg_a, w_rg_x))
    q_scale = LOG2E / math.sqrt(d // SB_HEADS)
    qkv_scale = jnp.concatenate([jnp.full((d,), q_scale, F32), jnp.ones((2 * d,), F32)])
    for b in range(bsz):
        h = x[b]
        for i in range(depth):
            j = i // 2
            if i % 2 == 0:
                u = _norm_matmul(h, norm_mix_g[i], w_in_rec, j, F32)
                y = _recmix(u, conv_a_w[j], conv_a_b[j], ln_a_g[j], ln_a_b[j], conv_b_w[j], conv_b_b[j],
                            w_rg_a, b_rg_a[j], w_rg_x, b_rg_x[j], rg_lambda[j], j)
                h = _matmul_residual(y, w_out_rec, j, h)
            else:
                qkv = _norm_matmul(h, norm_mix_g[i], w_qkv, j, BF16, col_scale=qkv_scale)
                o = _attention(qkv, SB_HEADS)
                h = _matmul_residual(o, w_o_attn, j, h)
            h = _mlp(h, norm_mlp_g[i], w_mlp_up, w_mlp_down, i)
            h = _ple(h, norm_ple_g[i], w_ple_gate, p, w_ple_proj, i, b, norm_f_g,
                     final_norm=(i == depth - 1))
        outs.append(h)
    return jnp.stack(outs, axis=0)
```

```python
import functools
import math

import jax
import jax.numpy as jnp
from jax import lax
from jax.experimental import pallas as pl
from jax.experimental.pallas import tpu as pltpu

F32 = jnp.float32
BF16 = jnp.bfloat16

EPS = 1e-6
RG_C = 8.0
RG_HEADS = 8
SB_HEADS = 16
CONV_A_HALO = 32
CONV_B_HALO = 8
LANES = 128
SUBLANES = 8
VMEM_LIMIT = 56 * 1024 * 1024


def _params(semantics):
    return pltpu.CompilerParams(dimension_semantics=semantics, vmem_limit_bytes=VMEM_LIMIT)


def _rms_normed(x, g):
    ms = jnp.mean(x * x, axis=-1, keepdims=True)
    return x * lax.rsqrt(ms + EPS) * g


def _sigmoid(x):
    return 1.0 / (1.0 + jnp.exp(-x))


def _softplus(x):
    return jnp.maximum(x, 0.0) + jnp.log(1.0 + jnp.exp(-jnp.abs(x)))


def _norm_matmul_kernel(x_ref, g_ref, w_ref, *rest, scaled):
    cs_ref = rest[0] if scaled else None
    o_ref, xn_ref = rest[-2:]

    @pl.when(pl.program_id(1) == 0)
    def _():
        xn_ref[...] = _rms_normed(x_ref[...], g_ref[...]).astype(BF16)

    acc = jnp.dot(xn_ref[...], w_ref[...], preferred_element_type=F32)
    if scaled:
        acc = acc * cs_ref[...]
    o_ref[...] = acc.astype(o_ref.dtype)


def _norm_matmul(x, g, w, layer, out_dtype, col_scale=None, tm=1024, tn=1024):
    m, d = x.shape
    n = w.shape[2]
    scaled = col_scale is not None
    in_specs = [
        pl.BlockSpec((tm, d), lambda i, j: (i, 0)),
        pl.BlockSpec((1, d), lambda i, j: (0, 0)),
        pl.BlockSpec((None, d, tn), lambda i, j: (layer, 0, j)),
    ]
    args = [x, g.reshape(1, d), w]
    if scaled:
        in_specs.append(pl.BlockSpec((1, tn), lambda i, j: (0, j)))
        args.append(col_scale.reshape(1, n))
    return pl.pallas_call(
        functools.partial(_norm_matmul_kernel, scaled=scaled),
        grid=(m // tm, n // tn),
        in_specs=in_specs,
        out_specs=pl.BlockSpec((tm, tn), lambda i, j: (i, j)),
        out_shape=jax.ShapeDtypeStruct((m, n), out_dtype),
        scratch_shapes=[pltpu.VMEM((tm, d), BF16)],
        compiler_params=_params(("parallel", "arbitrary")),
        name="norm_matmul",
    )(*args)


def _matmul_residual_kernel(y_ref, w_ref, h_ref, o_ref):
    o_ref[...] = h_ref[...] + jnp.dot(y_ref[...], w_ref[...], preferred_element_type=F32)


def _matmul_residual(y, w, layer, h, tm=512):
    m, k = y.shape
    n = w.shape[2]
    return pl.pallas_call(
        _matmul_residual_kernel,
        grid=(m // tm,),
        in_specs=[
            pl.BlockSpec((tm, k), lambda i: (i, 0)),
            pl.BlockSpec((None, k, n), lambda i: (layer, 0, 0)),
            pl.BlockSpec((tm, n), lambda i: (i, 0)),
        ],
        out_specs=pl.BlockSpec((tm, n), lambda i: (i, 0)),
        out_shape=jax.ShapeDtypeStruct((m, n), F32),
        compiler_params=_params(("parallel",)),
        name="matmul_residual",
    )(y, w, h)


def _mlp_kernel(h_ref, g_ref, wu_ref, wd_ref, o_ref, xn_ref):
    @pl.when(pl.program_id(1) == 0)
    def _():
        h = h_ref[...]
        xn_ref[...] = _rms_normed(h, g_ref[...]).astype(BF16)
        o_ref[...] = h

    a = jnp.dot(xn_ref[...], wu_ref[...].astype(BF16), preferred_element_type=F32)
    a = jnp.maximum(a, 0.0)
    a = (a * a).astype(BF16)
    o_ref[...] += jnp.dot(a, wd_ref[...].astype(BF16), preferred_element_type=F32)


def _mlp(h, g, w_up, w_down, layer, tm=1024, tf=512):
    m, d = h.shape
    f = w_up.shape[2]
    return pl.pallas_call(
        _mlp_kernel,
        grid=(m // tm, f // tf),
        in_specs=[
            pl.BlockSpec((tm, d), lambda i, j: (i, 0)),
            pl.BlockSpec((1, d), lambda i, j: (0, 0)),
            pl.BlockSpec((None, d, tf), lambda i, j: (layer, 0, j)),
            pl.BlockSpec((None, tf, d), lambda i, j: (layer, j, 0)),
        ],
        out_specs=pl.BlockSpec((tm, d), lambda i, j: (i, 0)),
        out_shape=jax.ShapeDtypeStruct((m, d), F32),
        scratch_shapes=[pltpu.VMEM((tm, d), BF16)],
        compiler_params=_params(("parallel", "arbitrary")),
        name="mlp",
    )(h, g.reshape(1, d), w_up, w_down)


def _ple_kernel(h_ref, g_ref, wg_ref, p_ref, wp_ref, gf_ref, o_ref, *, final_norm):
    h = h_ref[...]
    xn = _rms_normed(h, g_ref[...]).astype(BF16)
    gate = _sigmoid(jnp.dot(xn, wg_ref[...], preferred_element_type=F32))
    emb = jnp.dot(p_ref[...].astype(BF16), wp_ref[...], preferred_element_type=F32)
    out = h + emb * gate
    if final_norm:
        out = _rms_normed(out, gf_ref[...])
    o_ref[...] = out


def _ple(h, g, w_gate, p, w_proj, layer, batch, g_final, final_norm, tm=256):
    m, d = h.shape
    pd = p.shape[3]
    return pl.pallas_call(
        functools.partial(_ple_kernel, final_norm=final_norm),
        grid=(m // tm,),
        in_specs=[
            pl.BlockSpec((tm, d), lambda i: (i, 0)),
            pl.BlockSpec((1, d), lambda i: (0, 0)),
            pl.BlockSpec((None, d, d), lambda i: (layer, 0, 0)),
            pl.BlockSpec((None, None, tm, pd), lambda i: (layer, batch, i, 0)),
            pl.BlockSpec((None, pd, d), lambda i: (layer, 0, 0)),
            pl.BlockSpec((1, d), lambda i: (0, 0)),
        ],
        out_specs=pl.BlockSpec((tm, d), lambda i: (i, 0)),
        out_shape=jax.ShapeDtypeStruct((m, d), F32),
        compiler_params=_params(("parallel",)),
        name="ple",
    )(h, g.reshape(1, d), w_gate, p, w_proj, g_final.reshape(1, d))


def _shift_rows(x, d, fill):
    rolled = pltpu.roll(x, d, 0)
    row = lax.broadcasted_iota(jnp.int32, x.shape, 0)
    return jnp.where(row >= d, rolled, fill)


def _recmix_kernel(u_ref, caw_ref, cab_ref, lng_ref, lnb_ref, cbw_ref, cbb_ref,
                   wa_ref, ba_ref, wx_ref, bx_ref, lam_ref, o_ref,
                   glu_buf, conv_buf, xr_buf, hc_ref, *, ts, ch):
    n_chunks = ch // LANES
    ka = caw_ref.shape[0]
    kb = cbw_ref.shape[0]

    @pl.when(pl.program_id(0) == 0)
    def _():
        glu_buf[0:CONV_A_HALO, :] = jnp.zeros((CONV_A_HALO, ch), F32)
        xr_buf[0:CONV_B_HALO, :] = jnp.zeros((CONV_B_HALO, ch), F32)
        hc_ref[...] = jnp.zeros(hc_ref.shape, F32)

    glu_buf[CONV_A_HALO:CONV_A_HALO + ts, :] = u_ref[:, 0:ch] * _sigmoid(u_ref[:, ch:2 * ch])
    offs = [CONV_A_HALO - (ka - 1) + k for k in range(ka)]
    rows = ts + CONV_A_HALO
    for c in range(n_chunks):
        cs = slice(c * LANES, (c + 1) * LANES)
        hist = glu_buf[:, cs]
        acc = jnp.broadcast_to(cab_ref[:, cs], (ts, LANES))
        for r in range(SUBLANES):
            taps = [k for k in range(ka) if offs[k] % SUBLANES == r]
            if not taps:
                continue
            shifted = hist if r == 0 else pltpu.roll(hist, rows - r, 0)
            for k in taps:
                acc = acc + caw_ref[k:k + 1, cs] * shifted[offs[k] - r:offs[k] - r + ts, :]
        conv_buf[:, cs] = acc
    glu_buf[0:CONV_A_HALO, :] = glu_buf[ts:ts + CONV_A_HALO, :]

    ya = conv_buf[...]
    mu = jnp.mean(ya, axis=-1, keepdims=True)
    yc = ya - mu
    var = jnp.mean(yc * yc, axis=-1, keepdims=True)
    yn = yc * lax.rsqrt(var + EPS) * lng_ref[...] + lnb_ref[...]
    o_ref[:, 0:ch] = (yn * _sigmoid(yn)).astype(o_ref.dtype)

    xr_buf[CONV_B_HALO:CONV_B_HALO + ts, :] = u_ref[:, 2 * ch:3 * ch]
    for c in range(n_chunks):
        cs = slice(c * LANES, (c + 1) * LANES)
        xc = jnp.broadcast_to(cbb_ref[:, cs], (ts, LANES))
        for k in range(kb):
            off = CONV_B_HALO - (kb - 1) + k
            xc = xc + cbw_ref[k:k + 1, cs] * xr_buf[off:off + ts, cs]
        xcb = xc.astype(BF16)
        r = _sigmoid(jnp.dot(xcb, wa_ref[c], preferred_element_type=F32) + ba_ref[:, cs])
        gi = _sigmoid(jnp.dot(xcb, wx_ref[c], preferred_element_type=F32) + bx_ref[:, cs])
        log_a = (-RG_C) * r * _softplus(-lam_ref[:, cs])
        a = jnp.exp(log_a)
        uu = jnp.sqrt(1.0 - jnp.exp(2.0 * log_a)) * (gi * xc)
        d = 1
        while d < ts:
            a_prev = _shift_rows(a, d, 1.0)
            u_prev = _shift_rows(uu, d, 0.0)
            uu = a * u_prev + uu
            a = a * a_prev
            d *= 2
        h = uu + a * hc_ref[:, cs]
        hc_ref[:, cs] = h[ts - 1:ts, :]
        gr = u_ref[:, 3 * ch + c * LANES:3 * ch + (c + 1) * LANES]
        gelu = 0.5 * gr * (1.0 + jnp.tanh(math.sqrt(2.0 / math.pi) * (gr + 0.044715 * (gr * gr * gr))))
        o_ref[:, ch + c * LANES:ch + (c + 1) * LANES] = (h * gelu).astype(o_ref.dtype)
    xr_buf[0:CONV_B_HALO, :] = xr_buf[ts:ts + CONV_B_HALO, :]


def _recmix(u, caw, cab, lng, lnb, cbw, cbb, wa, ba, wx, bx, lam, layer, ts=256):
    s, w4 = u.shape
    ch = w4 // 4
    row = lambda v: v.reshape(1, ch)
    const2 = lambda i: (0, 0)
    gate_spec = pl.BlockSpec((None,) + wa.shape[1:], lambda i: (layer, 0, 0, 0))
    return pl.pallas_call(
        functools.partial(_recmix_kernel, ts=ts, ch=ch),
        grid=(s // ts,),
        in_specs=[
            pl.BlockSpec((ts, w4), lambda i: (i, 0)),
            pl.BlockSpec(caw.shape, const2),
            pl.BlockSpec((1, ch), const2),
            pl.BlockSpec((1, ch), const2),
            pl.BlockSpec((1, ch), const2),
            pl.BlockSpec(cbw.shape, const2),
            pl.BlockSpec((1, ch), const2),
            gate_spec,
            pl.BlockSpec((1, ch), const2),
            gate_spec,
            pl.BlockSpec((1, ch), const2),
            pl.BlockSpec((1, ch), const2),
        ],
        out_specs=pl.BlockSpec((ts, 2 * ch), lambda i: (i, 0)),
        out_shape=jax.ShapeDtypeStruct((s, 2 * ch), BF16),
        scratch_shapes=[
            pltpu.VMEM((ts + CONV_A_HALO, ch), F32),
            pltpu.VMEM((ts, ch), F32),
            pltpu.VMEM((ts + CONV_B_HALO, ch), F32),
            pltpu.VMEM((1, ch), F32),
        ],
        compiler_params=_params(("arbitrary",)),
        name="recmix",
    )(u, caw, row(cab), row(lng), row(lnb), cbw, row(cbb), wa, row(ba), wx, row(bx), row(lam))


ATTN_HEADS_PER_STEP = 4
ATTN_KEY_BLOCKS_PER_TRIP = 2
LOG2E = 1.4426950408889634
SIGN_BIT = -2147483648


def _attn_kernel(q_ref, k_ref, v_ref, ntri_ref, o_ref, *, bq, bk, dh):
    qi = pl.program_id(1)
    ntri = ntri_ref[...]
    nt = (((1,), (1,)), ((), ()))
    heads = range(ATTN_HEADS_PER_STEP)
    step = ATTN_KEY_BLOCKS_PER_TRIP
    qs = [q_ref[:, hh * dh:(hh + 1) * dh] for hh in heads]
    todo = [(hh, n) for hh in heads for n in range(step)]

    def block_ref(ref, hh, kb):
        return ref[pl.ds(pl.multiple_of(kb * bk, bk), bk), hh * dh:(hh + 1) * dh]

    def scores(kbs):
        return [lax.dot_general(qs[hh], block_ref(k_ref, hh, kbs[n]), nt, preferred_element_type=F32)
                for hh, n in todo]

    def logits(zs, masks):
        xs, rowsums = [], []
        for c, (hh, n) in enumerate(todo):
            z = zs[c]
            t = lax.bitcast_convert_type(lax.bitcast_convert_type(z, jnp.int32) | SIGN_BIT, F32)
            sp = jnp.maximum(z, 0.0) + jnp.log(1.0 + jnp.exp2(t)) * LOG2E
            if masks is not None:
                sp = jnp.where(masks[n], sp, 0.0)
            tail = jnp.dot(sp.astype(BF16), ntri, preferred_element_type=F32)
            xs.append(z + tail)
            rowsums.append(tail[:, 0:1])
        return xs, rowsums

    def weighted_values(kbs, xs, masks):
        pvs = []
        for c, (hh, n) in enumerate(todo):
            w = jnp.exp2(xs[c])
            if masks is not None:
                w = jnp.where(masks[n], w, 0.0)
            pvs.append(jnp.dot(w.astype(BF16), block_ref(v_ref, hh, kbs[n]), preferred_element_type=F32))
        return pvs

    def accumulate(state, pvs, rowsums):
        new = []
        for hh in heads:
            carry, acc = state[hh]
            for c, (h2, n) in enumerate(todo):
                if h2 == hh:
                    acc = acc + jnp.exp2(carry) * pvs[c]
                    carry = carry + rowsums[c]
            new.append((carry, acc))
        return tuple(new)

    def sweep(state, kbs, masks=None):
        xs, rowsums = logits(scores(kbs), masks)
        return accumulate(state, weighted_values(kbs, xs, masks), rowsums)

    row = lax.broadcasted_iota(jnp.int32, (bq, bk), 0)
    col = lax.broadcasted_iota(jnp.int32, (bq, bk), 1)
    offsets = list(reversed(range(step)))
    zero = (jnp.zeros((bq, 1), F32), jnp.zeros((bq, dh), F32))
    state = sweep(tuple(zero for _ in heads), [step * qi + n for n in offsets],
                  [col + n * bk < row for n in offsets])

    def older(j, state):
        newest = step * (qi - j) - 1
        return sweep(state, [newest - n for n in range(step)])

    state = lax.fori_loop(0, qi, older, state)
    for hh in heads:
        o_ref[:, hh * dh:(hh + 1) * dh] = state[hh][1].astype(o_ref.dtype)


def _attention(qkv, n_heads, bq=512, bk=256):
    s, w3 = qkv.shape
    dh = w3 // (3 * n_heads)
    hps = ATTN_HEADS_PER_STEP
    groups = n_heads // hps
    assert bq == ATTN_KEY_BLOCKS_PER_TRIP * bk
    tri = (jnp.arange(bk)[:, None] >= jnp.arange(bk)[None, :])
    ntri = -tri.astype(BF16)
    return pl.pallas_call(
        functools.partial(_attn_kernel, bq=bq, bk=bk, dh=dh),
        grid=(groups, s // bq),
        in_specs=[
            pl.BlockSpec((bq, hps * dh), lambda g, i: (i, g)),
            pl.BlockSpec((s, hps * dh), lambda g, i: (0, groups + g)),
            pl.BlockSpec((s, hps * dh), lambda g, i: (0, 2 * groups + g)),
            pl.BlockSpec((bk, bk), lambda g, i: (0, 0)),
        ],
        out_specs=pl.BlockSpec((bq, hps * dh), lambda g, i: (i, g)),
        out_shape=jax.ShapeDtypeStruct((s, n_heads * dh), BF16),
        compiler_params=_params(("parallel", "arbitrary")),
        name="sb_attention",
    )(qkv, qkv, qkv, ntri)


def kernel(x, p, norm_mix_g, norm_mlp_g, norm_ple_g, norm_f_g, w_in_rec, conv_a_w, conv_a_b, ln_a_g, ln_a_b, conv_b_w, conv_b_b, w_rg_a, b_rg_a, w_rg_x, b_rg_x, rg_lambda, w_out_rec, w_qkv, w_o_attn, w_mlp_up, w_mlp_down, w_ple_proj, w_ple_gate):
    bsz, seq, d = x.shape
    depth = p.shape[0]
    outs = []
    w_in_rec, w_out_rec, w_qkv, w_o_attn, w_ple_proj, w_ple_gate, w_rg_a, w_rg_x = (
        w.astype(BF16) for w in (w_in_rec, w_out_rec, w_qkv, w_o_attn,
                                 w_ple_proj, w_ple_gate, w_rg_a, w_rg_x))
    q_scale = LOG2E / math.sqrt(d // SB_HEADS)
    qkv_scale = jnp.concatenate([jnp.full((d,), q_scale, F32), jnp.ones((2 * d,), F32)])
    for b in range(bsz):
        h = x[b]
        for i in range(depth):
            j = i // 2
            if i % 2 == 0:
                u = _norm_matmul(h, norm_mix_g[i], w_in_rec, j, F32)
                y = _recmix(u, conv_a_w[j], conv_a_b[j], ln_a_g[j], ln_a_b[j], conv_b_w[j], conv_b_b[j],
                            w_rg_a, b_rg_a[j], w_rg_x, b_rg_x[j], rg_lambda[j], j)
                h = _matmul_residual(y, w_out_rec, j, h)
            else:
                qkv = _norm_matmul(h, norm_mix_g[i], w_qkv, j, BF16, col_scale=qkv_scale)
                o = _attention(qkv, SB_HEADS)
                h = _matmul_residual(o, w_o_attn, j, h)
            h = _mlp(h, norm_mlp_g[i], w_mlp_up, w_mlp_down, i)
            h = _ple(h, norm_ple_g[i], w_ple_gate, p, w_ple_proj, i, b, norm_f_g,
                     final_norm=(i == depth - 1))
        outs.append(h)
    return jnp.stack(outs, axis=0)
```

```python
import functools
import math

import jax
import jax.numpy as jnp
from jax import lax
from jax.experimental import pallas as pl
from jax.experimental.pallas import tpu as pltpu

F32 = jnp.float32
BF16 = jnp.bfloat16

EPS = 1e-6
RG_C = 8.0
RG_HEADS = 8
SB_HEADS = 16
CONV_A_HALO = 32
CONV_B_HALO = 8
LANES = 128
SUBLANES = 8
VMEM_LIMIT = 56 * 1024 * 1024


def _params(semantics):
    return pltpu.CompilerParams(dimension_semantics=semantics, vmem_limit_bytes=VMEM_LIMIT)


def _rms_normed(x, g):
    ms = jnp.mean(x * x, axis=-1, keepdims=True)
    return x * lax.rsqrt(ms + EPS) * g


def _sigmoid(x):
    return 1.0 / (1.0 + jnp.exp(-x))


def _softplus(x):
    return jnp.maximum(x, 0.0) + jnp.log(1.0 + jnp.exp(-jnp.abs(x)))


def _norm_matmul_kernel(x_ref, g_ref, w_ref, *rest, scaled):
    cs_ref = rest[0] if scaled else None
    o_ref, xn_ref = rest[-2:]

    @pl.when(pl.program_id(1) == 0)
    def _():
        xn_ref[...] = _rms_normed(x_ref[...], g_ref[...]).astype(BF16)

    acc = jnp.dot(xn_ref[...], w_ref[...], preferred_element_type=F32)
    if scaled:
        acc = acc * cs_ref[...]
    o_ref[...] = acc.astype(o_ref.dtype)


def _norm_matmul(x, g, w, layer, out_dtype, col_scale=None, tm=1024, tn=1024):
    m, d = x.shape
    n = w.shape[2]
    scaled = col_scale is not None
    in_specs = [
        pl.BlockSpec((tm, d), lambda i, j: (i, 0)),
        pl.BlockSpec((1, d), lambda i, j: (0, 0)),
        pl.BlockSpec((None, d, tn), lambda i, j: (layer, 0, j)),
    ]
    args = [x, g.reshape(1, d), w]
    if scaled:
        in_specs.append(pl.BlockSpec((1, tn), lambda i, j: (0, j)))
        args.append(col_scale.reshape(1, n))
    return pl.pallas_call(
        functools.partial(_norm_matmul_kernel, scaled=scaled),
        grid=(m // tm, n // tn),
        in_specs=in_specs,
        out_specs=pl.BlockSpec((tm, tn), lambda i, j: (i, j)),
        out_shape=jax.ShapeDtypeStruct((m, n), out_dtype),
        scratch_shapes=[pltpu.VMEM((tm, d), BF16)],
        compiler_params=_params(("parallel", "arbitrary")),
        name="norm_matmul",
    )(*args)


def _matmul_residual_kernel(y_ref, w_ref, h_ref, o_ref):
    o_ref[...] = h_ref[...] + jnp.dot(y_ref[...], w_ref[...], preferred_element_type=F32)


def _matmul_residual(y, w, layer, h, tm=512):
    m, k = y.shape
    n = w.shape[2]
    return pl.pallas_call(
        _matmul_residual_kernel,
        grid=(m // tm,),
        in_specs=[
            pl.BlockSpec((tm, k), lambda i: (i, 0)),
            pl.BlockSpec((None, k, n), lambda i: (layer, 0, 0)),
            pl.BlockSpec((tm, n), lambda i: (i, 0)),
        ],
        out_specs=pl.BlockSpec((tm, n), lambda i: (i, 0)),
        out_shape=jax.ShapeDtypeStruct((m, n), F32),
        compiler_params=_params(("parallel",)),
        name="matmul_residual",
    )(y, w, h)


def _mlp_kernel(h_ref, g_ref, wu_ref, wd_ref, o_ref, xn_ref):
    @pl.when(pl.program_id(1) == 0)
    def _():
        h = h_ref[...]
        xn_ref[...] = _rms_normed(h, g_ref[...]).astype(BF16)
        o_ref[...] = h

    a = jnp.dot(xn_ref[...], wu_ref[...].astype(BF16), preferred_element_type=F32)
    a = jnp.maximum(a, 0.0)
    a = (a * a).astype(BF16)
    o_ref[...] += jnp.dot(a, wd_ref[...].astype(BF16), preferred_element_type=F32)


def _mlp(h, g, w_up, w_down, layer, tm=1024, tf=512):
    m, d = h.shape
    f = w_up.shape[2]
    return pl.pallas_call(
        _mlp_kernel,
        grid=(m // tm, f // tf),
        in_specs=[
            pl.BlockSpec((tm, d), lambda i, j: (i, 0)),
            pl.BlockSpec((1, d), lambda i, j: (0, 0)),
            pl.BlockSpec((None, d, tf), lambda i, j: (layer, 0, j)),
            pl.BlockSpec((None, tf, d), lambda i, j: (layer, j, 0)),
        ],
        out_specs=pl.BlockSpec((tm, d), lambda i, j: (i, 0)),
        out_shape=jax.ShapeDtypeStruct((m, d), F32),
        scratch_shapes=[pltpu.VMEM((tm, d), BF16)],
        compiler_params=_params(("parallel", "arbitrary")),
        name="mlp",
    )(h, g.reshape(1, d), w_up, w_down)


def _ple_kernel(h_ref, g_ref, wg_ref, p_ref, wp_ref, gf_ref, o_ref, *, final_norm):
    h = h_ref[...]
    xn = _rms_normed(h, g_ref[...]).astype(BF16)
    gate = _sigmoid(jnp.dot(xn, wg_ref[...], preferred_element_type=F32))
    emb = jnp.dot(p_ref[...].astype(BF16), wp_ref[...], preferred_element_type=F32)
    out = h + emb * gate
    if final_norm:
        out = _rms_normed(out, gf_ref[...])
    o_ref[...] = out


def _ple(h, g, w_gate, p, w_proj, layer, batch, g_final, final_norm, tm=256):
    m, d = h.shape
    pd = p.shape[3]
    return pl.pallas_call(
        functools.partial(_ple_kernel, final_norm=final_norm),
        grid=(m // tm,),
        in_specs=[
            pl.BlockSpec((tm, d), lambda i: (i, 0)),
            pl.BlockSpec((1, d), lambda i: (0, 0)),
            pl.BlockSpec((None, d, d), lambda i: (layer, 0, 0)),
            pl.BlockSpec((None, None, tm, pd), lambda i: (layer, batch, i, 0)),
            pl.BlockSpec((None, pd, d), lambda i: (layer, 0, 0)),
            pl.BlockSpec((1, d), lambda i: (0, 0)),
        ],
        out_specs=pl.BlockSpec((tm, d), lambda i: (i, 0)),
        out_shape=jax.ShapeDtypeStruct((m, d), F32),
        compiler_params=_params(("parallel",)),
        name="ple",
    )(h, g.reshape(1, d), w_gate, p, w_proj, g_final.reshape(1, d))


def _shift_rows(x, d, fill):
    rolled = pltpu.roll(x, d, 0)
    row = lax.broadcasted_iota(jnp.int32, x.shape, 0)
    return jnp.where(row >= d, rolled, fill)


def _recmix_kernel(u_ref, caw_ref, cab_ref, lng_ref, lnb_ref, cbw_ref, cbb_ref,
                   wa_ref, ba_ref, wx_ref, bx_ref, lam_ref, o_ref,
                   glu_buf, conv_buf, xr_buf, hc_ref, *, ts, ch):
    n_chunks = ch // LANES
    ka = caw_ref.shape[0]
    kb = cbw_ref.shape[0]

    @pl.when(pl.program_id(0) == 0)
    def _():
        glu_buf[0:CONV_A_HALO, :] = jnp.zeros((CONV_A_HALO, ch), F32)
        xr_buf[0:CONV_B_HALO, :] = jnp.zeros((CONV_B_HALO, ch), F32)
        hc_ref[...] = jnp.zeros(hc_ref.shape, F32)

    glu_buf[CONV_A_HALO:CONV_A_HALO + ts, :] = u_ref[:, 0:ch] * _sigmoid(u_ref[:, ch:2 * ch])
    offs = [CONV_A_HALO - (ka - 1) + k for k in range(ka)]
    rows = ts + CONV_A_HALO
    for c in range(n_chunks):
        cs = slice(c * LANES, (c + 1) * LANES)
        hist = glu_buf[:, cs]
        acc = jnp.broadcast_to(cab_ref[:, cs], (ts, LANES))
        for r in range(SUBLANES):
            taps = [k for k in range(ka) if offs[k] % SUBLANES == r]
            if not taps:
                continue
            shifted = hist if r == 0 else pltpu.roll(hist, rows - r, 0)
            for k in taps:
                acc = acc + caw_ref[k:k + 1, cs] * shifted[offs[k] - r:offs[k] - r + ts, :]
        conv_buf[:, cs] = acc
    glu_buf[0:CONV_A_HALO, :] = glu_buf[ts:ts + CONV_A_HALO, :]

    ya = conv_buf[...]
    mu = jnp.mean(ya, axis=-1, keepdims=True)
    yc = ya - mu
    var = jnp.mean(yc * yc, axis=-1, keepdims=True)
    yn = yc * lax.rsqrt(var + EPS) * lng_ref[...] + lnb_ref[...]
    o_ref[:, 0:ch] = (yn * _sigmoid(yn)).astype(o_ref.dtype)

    xr_buf[CONV_B_HALO:CONV_B_HALO + ts, :] = u_ref[:, 2 * ch:3 * ch]
    for c in range(n_chunks):
        cs = slice(c * LANES, (c + 1) * LANES)
        xc = jnp.broadcast_to(cbb_ref[:, cs], (ts, LANES))
        for k in range(kb):
            off = CONV_B_HALO - (kb - 1) + k
            xc = xc + cbw_ref[k:k + 1, cs] * xr_buf[off:off + ts, cs]
        xcb = xc.astype(BF16)
        r = _sigmoid(jnp.dot(xcb, wa_ref[c], preferred_element_type=F32) + ba_ref[:, cs])
        gi = _sigmoid(jnp.dot(xcb, wx_ref[c], preferred_element_type=F32) + bx_ref[:, cs])
        log_a = (-RG_C) * r * _softplus(-lam_ref[:, cs])
        a = jnp.exp(log_a)
        uu = jnp.sqrt(1.0 - jnp.exp(2.0 * log_a)) * (gi * xc)
        d = 1
        while d < ts:
            a_prev = _shift_rows(a, d, 1.0)
            u_prev = _shift_rows(uu, d, 0.0)
            uu = a * u_prev + uu
            a = a * a_prev
            d *= 2
        h = uu + a * hc_ref[:, cs]
        hc_ref[:, cs] = h[ts - 1:ts, :]
        gr = u_ref[:, 3 * ch + c * LANES:3 * ch + (c + 1) * LANES]
        gelu = 0.5 * gr * (1.0 + jnp.tanh(math.sqrt(2.0 / math.pi) * (gr + 0.044715 * (gr * gr * gr))))
        o_ref[:, ch + c * LANES:ch + (c + 1) * LANES] = (h * gelu).astype(o_ref.dtype)
    xr_buf[0:CONV_B_HALO, :] = xr_buf[ts:ts + CONV_B_HALO, :]


def _recmix(u, caw, cab, lng, lnb, cbw, cbb, wa, ba, wx, bx, lam, layer, ts=256):
    s, w4 = u.shape
    ch = w4 // 4
    row = lambda v: v.reshape(1, ch)
    const2 = lambda i: (0, 0)
    gate_spec = pl.BlockSpec((None,) + wa.shape[1:], lambda i: (layer, 0, 0, 0))
    return pl.pallas_call(
        functools.partial(_recmix_kernel, ts=ts, ch=ch),
        grid=(s // ts,),
        in_specs=[
            pl.BlockSpec((ts, w4), lambda i: (i, 0)),
            pl.BlockSpec(caw.shape, const2),
            pl.BlockSpec((1, ch), const2),
            pl.BlockSpec((1, ch), const2),
            pl.BlockSpec((1, ch), const2),
            pl.BlockSpec(cbw.shape, const2),
            pl.BlockSpec((1, ch), const2),
            gate_spec,
            pl.BlockSpec((1, ch), const2),
            gate_spec,
            pl.BlockSpec((1, ch), const2),
            pl.BlockSpec((1, ch), const2),
        ],
        out_specs=pl.BlockSpec((ts, 2 * ch), lambda i: (i, 0)),
        out_shape=jax.ShapeDtypeStruct((s, 2 * ch), BF16),
        scratch_shapes=[
            pltpu.VMEM((ts + CONV_A_HALO, ch), F32),
            pltpu.VMEM((ts, ch), F32),
            pltpu.VMEM((ts + CONV_B_HALO, ch), F32),
            pltpu.VMEM((1, ch), F32),
        ],
        compiler_params=_params(("arbitrary",)),
        name="recmix",
    )(u, caw, row(cab), row(lng), row(lnb), cbw, row(cbb), wa, row(ba), wx, row(bx), row(lam))


ATTN_HEADS_PER_STEP = 4
ATTN_KEY_BLOCKS_PER_TRIP = 2
LOG2E = 1.4426950408889634
SIGN_BIT = -2147483648


def _attn_kernel(q_ref, k_ref, v_ref, ntri_ref, o_ref, acc_ref, carry_ref, *, bq, bk, dh):
    qi = pl.program_id(1)
    ntri = ntri_ref[...]
    nt = (((1,), (1,)), ((), ()))
    heads = range(ATTN_HEADS_PER_STEP)
    step = ATTN_KEY_BLOCKS_PER_TRIP

    def block_ref(ref, hh, kb):
        return ref[pl.ds(pl.multiple_of(kb * bk, bk), bk), hh * dh:(hh + 1) * dh]

    def sweep(kbs, masks=None, rows=slice(None)):
        masks = masks or [None] * len(kbs)
        zs = [[lax.dot_general(q_ref[rows, hh * dh:(hh + 1) * dh], block_ref(k_ref, hh, kb), nt,
                               preferred_element_type=F32) for kb in kbs] for hh in heads]
        xs, sums = [], []
        for hh in heads:
            newer = None
            row = []
            for n in range(len(kbs)):
                z = zs[hh][n]
                t = lax.bitcast_convert_type(lax.bitcast_convert_type(z, jnp.int32) | SIGN_BIT, F32)
                sp = jnp.maximum(z, 0.0) + jnp.log(1.0 + jnp.exp2(t)) * LOG2E
                if masks[n] is not None:
                    sp = jnp.where(masks[n], sp, 0.0)
                tail = jnp.dot(sp.astype(BF16), ntri, preferred_element_type=F32)
                x = z + tail
                if newer is not None:
                    x = x + newer
                row.append(x)
                newer = tail[:, 0:1] if newer is None else newer + tail[:, 0:1]
            xs.append(row)
            sums.append(newer)
        v_rows = pl.ds(pl.multiple_of(kbs[-1] * bk, bk), len(kbs) * bk)
        for hh in heads:
            ws = []
            for n in reversed(range(len(kbs))):
                w = jnp.exp2(xs[hh][n])
                if masks[n] is not None:
                    w = jnp.where(masks[n], w, 0.0)
                ws.append(w.astype(BF16))
            pv = jnp.dot(jnp.concatenate(ws, axis=1), v_ref[v_rows, hh * dh:(hh + 1) * dh],
                         preferred_element_type=F32)
            acc_ref[hh, rows] = acc_ref[hh, rows] + jnp.exp2(carry_ref[hh, rows]) * pv
            carry_ref[hh, rows] = carry_ref[hh, rows] + sums[hh]

    acc_ref[...] = jnp.zeros(acc_ref.shape, F32)
    carry_ref[...] = jnp.zeros(carry_ref.shape, F32)

    below_diagonal = (lax.broadcasted_iota(jnp.int32, (bk, bk), 1)
                      < lax.broadcasted_iota(jnp.int32, (bk, bk), 0))
    for r in range(step):
        sweep([step * qi + r - m for m in range(r + 1)], [below_diagonal] + [None] * r,
              rows=slice(r * bk, (r + 1) * bk))

    def older(j, _):
        newest = step * (qi - j) - 1
        sweep([newest - n for n in range(step)])
        return 0

    lax.fori_loop(0, qi, older, 0)

    for hh in heads:
        o_ref[:, hh * dh:(hh + 1) * dh] = acc_ref[hh].astype(o_ref.dtype)


def _attention(qkv, n_heads, bq=512, bk=256):
    s, w3 = qkv.shape
    dh = w3 // (3 * n_heads)
    hps = ATTN_HEADS_PER_STEP
    groups = n_heads // hps
    assert bq == ATTN_KEY_BLOCKS_PER_TRIP * bk
    tri = (jnp.arange(bk)[:, None] >= jnp.arange(bk)[None, :])
    ntri = -tri.astype(BF16)
    return pl.pallas_call(
        functools.partial(_attn_kernel, bq=bq, bk=bk, dh=dh),
        grid=(groups, s // bq),
        in_specs=[
            pl.BlockSpec((bq, hps * dh), lambda g, i: (i, g)),
            pl.BlockSpec((s, hps * dh), lambda g, i: (0, groups + g)),
            pl.BlockSpec((s, hps * dh), lambda g, i: (0, 2 * groups + g)),
            pl.BlockSpec((bk, bk), lambda g, i: (0, 0)),
        ],
        out_specs=pl.BlockSpec((bq, hps * dh), lambda g, i: (i, g)),
        out_shape=jax.ShapeDtypeStruct((s, n_heads * dh), BF16),
        scratch_shapes=[pltpu.VMEM((hps, bq, dh), F32), pltpu.VMEM((hps, bq, 1), F32)],
        compiler_params=_params(("parallel", "arbitrary")),
        name="sb_attention",
    )(qkv, qkv, qkv, ntri)


def kernel(x, p, norm_mix_g, norm_mlp_g, norm_ple_g, norm_f_g, w_in_rec, conv_a_w, conv_a_b, ln_a_g, ln_a_b, conv_b_w, conv_b_b, w_rg_a, b_rg_a, w_rg_x, b_rg_x, rg_lambda, w_out_rec, w_qkv, w_o_attn, w_mlp_up, w_mlp_down, w_ple_proj, w_ple_gate):
    bsz, seq, d = x.shape
    depth = p.shape[0]
    outs = []
    w_in_rec, w_out_rec, w_qkv, w_o_attn, w_ple_proj, w_ple_gate, w_rg_a, w_rg_x = (
        w.astype(BF16) for w in (w_in_rec, w_out_rec, w_qkv, w_o_attn,
                                 w_ple_proj, w_ple_gate, w_rg_a, w_rg_x))
    q_scale = LOG2E / math.sqrt(d // SB_HEADS)
    qkv_scale = jnp.concatenate([jnp.full((d,), q_scale, F32), jnp.ones((2 * d,), F32)])
    for b in range(bsz):
        h = x[b]
        for i in range(depth):
            j = i // 2
            if i % 2 == 0:
                u = _norm_matmul(h, norm_mix_g[i], w_in_rec, j, F32)
                y = _recmix(u, conv_a_w[j], conv_a_b[j], ln_a_g[j], ln_a_b[j], conv_b_w[j], conv_b_b[j],
                            w_rg_a, b_rg_a[j], w_rg_x, b_rg_x[j], rg_lambda[j], j)
                h = _matmul_residual(y, w_out_rec, j, h)
            else:
                qkv = _norm_matmul(h, norm_mix_g[i], w_qkv, j, BF16, col_scale=qkv_scale)
                o = _attention(qkv, SB_HEADS)
                h = _matmul_residual(o, w_o_attn, j, h)
            h = _mlp(h, norm_mlp_g[i], w_mlp_up, w_mlp_down, i)
            h = _ple(h, norm_ple_g[i], w_ple_gate, p, w_ple_proj, i, b, norm_f_g,
                     final_norm=(i == depth - 1))
        outs.append(h)
    return jnp.stack(outs, axis=0)
```

```python
import functools
import math

import jax
import jax.numpy as jnp
from jax import lax
from jax.experimental import pallas as pl
from jax.experimental.pallas import tpu as pltpu

F32 = jnp.float32
BF16 = jnp.bfloat16

EPS = 1e-6
RG_C = 8.0
RG_HEADS = 8
SB_HEADS = 16
CONV_A_HALO = 32
CONV_B_HALO = 8
LANES = 128
SUBLANES = 8
VMEM_LIMIT = 56 * 1024 * 1024


def _params(semantics):
    return pltpu.CompilerParams(dimension_semantics=semantics, vmem_limit_bytes=VMEM_LIMIT)


def _rms_normed(x, g):
    ms = jnp.mean(x * x, axis=-1, keepdims=True)
    return x * lax.rsqrt(ms + EPS) * g


def _sigmoid(x):
    return 1.0 / (1.0 + jnp.exp(-x))


def _softplus(x):
    return jnp.maximum(x, 0.0) + jnp.log(1.0 + jnp.exp(-jnp.abs(x)))


def _norm_matmul_kernel(x_ref, g_ref, w_ref, *rest, scaled):
    cs_ref = rest[0] if scaled else None
    o_ref, xn_ref = rest[-2:]

    @pl.when(pl.program_id(1) == 0)
    def _():
        xn_ref[...] = _rms_normed(x_ref[...], g_ref[...]).astype(BF16)

    acc = jnp.dot(xn_ref[...], w_ref[...], preferred_element_type=F32)
    if scaled:
        acc = acc * cs_ref[...]
    o_ref[...] = acc.astype(o_ref.dtype)


def _norm_matmul(x, g, w, layer, out_dtype, col_scale=None, tm=1024, tn=1024):
    m, d = x.shape
    n = w.shape[2]
    scaled = col_scale is not None
    in_specs = [
        pl.BlockSpec((tm, d), lambda i, j: (i, 0)),
        pl.BlockSpec((1, d), lambda i, j: (0, 0)),
        pl.BlockSpec((None, d, tn), lambda i, j: (layer, 0, j)),
    ]
    args = [x, g.reshape(1, d), w]
    if scaled:
        in_specs.append(pl.BlockSpec((1, tn), lambda i, j: (0, j)))
        args.append(col_scale.reshape(1, n))
    return pl.pallas_call(
        functools.partial(_norm_matmul_kernel, scaled=scaled),
        grid=(m // tm, n // tn),
        in_specs=in_specs,
        out_specs=pl.BlockSpec((tm, tn), lambda i, j: (i, j)),
        out_shape=jax.ShapeDtypeStruct((m, n), out_dtype),
        scratch_shapes=[pltpu.VMEM((tm, d), BF16)],
        compiler_params=_params(("parallel", "arbitrary")),
        name="norm_matmul",
    )(*args)


def _matmul_residual_kernel(y_ref, w_ref, h_ref, o_ref):
    o_ref[...] = h_ref[...] + jnp.dot(y_ref[...], w_ref[...], preferred_element_type=F32)


def _matmul_residual(y, w, layer, h, tm=512):
    m, k = y.shape
    n = w.shape[2]
    return pl.pallas_call(
        _matmul_residual_kernel,
        grid=(m // tm,),
        in_specs=[
            pl.BlockSpec((tm, k), lambda i: (i, 0)),
            pl.BlockSpec((None, k, n), lambda i: (layer, 0, 0)),
            pl.BlockSpec((tm, n), lambda i: (i, 0)),
        ],
        out_specs=pl.BlockSpec((tm, n), lambda i: (i, 0)),
        out_shape=jax.ShapeDtypeStruct((m, n), F32),
        compiler_params=_params(("parallel",)),
        name="matmul_residual",
    )(y, w, h)


def _mlp_kernel(h_ref, g_ref, wu_ref, wd_ref, o_ref, xn_ref):
    @pl.when(pl.program_id(1) == 0)
    def _():
        h = h_ref[...]
        xn_ref[...] = _rms_normed(h, g_ref[...]).astype(BF16)
        o_ref[...] = h

    a = jnp.dot(xn_ref[...], wu_ref[...].astype(BF16), preferred_element_type=F32)
    a = jnp.maximum(a, 0.0)
    a = (a * a).astype(BF16)
    o_ref[...] += jnp.dot(a, wd_ref[...].astype(BF16), preferred_element_type=F32)


def _mlp(h, g, w_up, w_down, layer, tm=1024, tf=512):
    m, d = h.shape
    f = w_up.shape[2]
    return pl.pallas_call(
        _mlp_kernel,
        grid=(m // tm, f // tf),
        in_specs=[
            pl.BlockSpec((tm, d), lambda i, j: (i, 0)),
            pl.BlockSpec((1, d), lambda i, j: (0, 0)),
            pl.BlockSpec((None, d, tf), lambda i, j: (layer, 0, j)),
            pl.BlockSpec((None, tf, d), lambda i, j: (layer, j, 0)),
        ],
        out_specs=pl.BlockSpec((tm, d), lambda i, j: (i, 0)),
        out_shape=jax.ShapeDtypeStruct((m, d), F32),
        scratch_shapes=[pltpu.VMEM((tm, d), BF16)],
        compiler_params=_params(("parallel", "arbitrary")),
        name="mlp",
    )(h, g.reshape(1, d), w_up, w_down)


def _ple_kernel(h_ref, g_ref, wg_ref, p_ref, wp_ref, gf_ref, o_ref, *, final_norm):
    h = h_ref[...]
    xn = _rms_normed(h, g_ref[...]).astype(BF16)
    gate = _sigmoid(jnp.dot(xn, wg_ref[...], preferred_element_type=F32))
    emb = jnp.dot(p_ref[...].astype(BF16), wp_ref[...], preferred_element_type=F32)
    out = h + emb * gate
    if final_norm:
        out = _rms_normed(out, gf_ref[...])
    o_ref[...] = out


def _ple(h, g, w_gate, p, w_proj, layer, batch, g_final, final_norm, tm=256):
    m, d = h.shape
    pd = p.shape[3]
    return pl.pallas_call(
        functools.partial(_ple_kernel, final_norm=final_norm),
        grid=(m // tm,),
        in_specs=[
            pl.BlockSpec((tm, d), lambda i: (i, 0)),
            pl.BlockSpec((1, d), lambda i: (0, 0)),
            pl.BlockSpec((None, d, d), lambda i: (layer, 0, 0)),
            pl.BlockSpec((None, None, tm, pd), lambda i: (layer, batch, i, 0)),
            pl.BlockSpec((None, pd, d), lambda i: (layer, 0, 0)),
            pl.BlockSpec((1, d), lambda i: (0, 0)),
        ],
        out_specs=pl.BlockSpec((tm, d), lambda i: (i, 0)),
        out_shape=jax.ShapeDtypeStruct((m, d), F32),
        compiler_params=_params(("parallel",)),
        name="ple",
    )(h, g.reshape(1, d), w_gate, p, w_proj, g_final.reshape(1, d))


def _shift_rows(x, d, fill):
    rolled = pltpu.roll(x, d, 0)
    row = lax.broadcasted_iota(jnp.int32, x.shape, 0)
    return jnp.where(row >= d, rolled, fill)


def _recmix_kernel(u_ref, caw_ref, cab_ref, lng_ref, lnb_ref, cbw_ref, cbb_ref,
                   wa_ref, ba_ref, wx_ref, bx_ref, lam_ref, o_ref,
                   glu_buf, conv_buf, xr_buf, hc_ref, *, ts, ch):
    n_chunks = ch // LANES
    ka = caw_ref.shape[0]
    kb = cbw_ref.shape[0]

    @pl.when(pl.program_id(0) == 0)
    def _():
        glu_buf[0:CONV_A_HALO, :] = jnp.zeros((CONV_A_HALO, ch), F32)
        xr_buf[0:CONV_B_HALO, :] = jnp.zeros((CONV_B_HALO, ch), F32)
        hc_ref[...] = jnp.zeros(hc_ref.shape, F32)

    glu_buf[CONV_A_HALO:CONV_A_HALO + ts, :] = u_ref[:, 0:ch] * _sigmoid(u_ref[:, ch:2 * ch])
    offs = [CONV_A_HALO - (ka - 1) + k for k in range(ka)]
    rows = ts + CONV_A_HALO
    for c in range(n_chunks):
        cs = slice(c * LANES, (c + 1) * LANES)
        hist = glu_buf[:, cs]
        acc = jnp.broadcast_to(cab_ref[:, cs], (ts, LANES))
        for r in range(SUBLANES):
            taps = [k for k in range(ka) if offs[k] % SUBLANES == r]
            if not taps:
                continue
            shifted = hist if r == 0 else pltpu.roll(hist, rows - r, 0)
            for k in taps:
                acc = acc + caw_ref[k:k + 1, cs] * shifted[offs[k] - r:offs[k] - r + ts, :]
        conv_buf[:, cs] = acc
    glu_buf[0:CONV_A_HALO, :] = glu_buf[ts:ts + CONV_A_HALO, :]

    ya = conv_buf[...]
    mu = jnp.mean(ya, axis=-1, keepdims=True)
    yc = ya - mu
    var = jnp.mean(yc * yc, axis=-1, keepdims=True)
    yn = yc * lax.rsqrt(var + EPS) * lng_ref[...] + lnb_ref[...]
    o_ref[:, 0:ch] = (yn * _sigmoid(yn)).astype(o_ref.dtype)

    xr_buf[CONV_B_HALO:CONV_B_HALO + ts, :] = u_ref[:, 2 * ch:3 * ch]
    for c in range(n_chunks):
        cs = slice(c * LANES, (c + 1) * LANES)
        xc = jnp.broadcast_to(cbb_ref[:, cs], (ts, LANES))
        for k in range(kb):
            off = CONV_B_HALO - (kb - 1) + k
            xc = xc + cbw_ref[k:k + 1, cs] * xr_buf[off:off + ts, cs]
        xcb = xc.astype(BF16)
        r = _sigmoid(jnp.dot(xcb, wa_ref[c], preferred_element_type=F32) + ba_ref[:, cs])
        gi = _sigmoid(jnp.dot(xcb, wx_ref[c], preferred_element_type=F32) + bx_ref[:, cs])
        log_a = (-RG_C) * r * _softplus(-lam_ref[:, cs])
        a = jnp.exp(log_a)
        uu = jnp.sqrt(1.0 - jnp.exp(2.0 * log_a)) * (gi * xc)
        d = 1
        while d < ts:
            a_prev = _shift_rows(a, d, 1.0)
            u_prev = _shift_rows(uu, d, 0.0)
            uu = a * u_prev + uu
            a = a * a_prev
            d *= 2
        h = uu + a * hc_ref[:, cs]
        hc_ref[:, cs] = h[ts - 1:ts, :]
        gr = u_ref[:, 3 * ch + c * LANES:3 * ch + (c + 1) * LANES]
        gelu = 0.5 * gr * (1.0 + jnp.tanh(math.sqrt(2.0 / math.pi) * (gr + 0.044715 * (gr * gr * gr))))
        o_ref[:, ch + c * LANES:ch + (c + 1) * LANES] = (h * gelu).astype(o_ref.dtype)
    xr_buf[0:CONV_B_HALO, :] = xr_buf[ts:ts + CONV_B_HALO, :]


def _recmix(u, caw, cab, lng, lnb, cbw, cbb, wa, ba, wx, bx, lam, layer, ts=256):
    s, w4 = u.shape
    ch = w4 // 4
    row = lambda v: v.reshape(1, ch)
    const2 = lambda i: (0, 0)
    gate_spec = pl.BlockSpec((None,) + wa.shape[1:], lambda i: (layer, 0, 0, 0))
    return pl.pallas_call(
        functools.partial(_recmix_kernel, ts=ts, ch=ch),
        grid=(s // ts,),
        in_specs=[
            pl.BlockSpec((ts, w4), lambda i: (i, 0)),
            pl.BlockSpec(caw.shape, const2),
            pl.BlockSpec((1, ch), const2),
            pl.BlockSpec((1, ch), const2),
            pl.BlockSpec((1, ch), const2),
            pl.BlockSpec(cbw.shape, const2),
            pl.BlockSpec((1, ch), const2),
            gate_spec,
            pl.BlockSpec((1, ch), const2),
            gate_spec,
            pl.BlockSpec((1, ch), const2),
            pl.BlockSpec((1, ch), const2),
        ],
        out_specs=pl.BlockSpec((ts, 2 * ch), lambda i: (i, 0)),
        out_shape=jax.ShapeDtypeStruct((s, 2 * ch), BF16),
        scratch_shapes=[
            pltpu.VMEM((ts + CONV_A_HALO, ch), F32),
            pltpu.VMEM((ts, ch), F32),
            pltpu.VMEM((ts + CONV_B_HALO, ch), F32),
            pltpu.VMEM((1, ch), F32),
        ],
        compiler_params=_params(("arbitrary",)),
        name="recmix",
    )(u, caw, row(cab), row(lng), row(lnb), cbw, row(cbb), wa, row(ba), wx, row(bx), row(lam))


ATTN_HEADS_PER_STEP = 4
ATTN_KEY_BLOCKS_PER_TRIP = 2
LOG2E = 1.4426950408889634
MAX_LOG2_SCORE = 126.0


def _attn_kernel(q_ref, k_ref, v_ref, nut_ref, o_ref, acc_ref, carry_ref, *, bq, bk, dh):
    qi = pl.program_id(1)
    nut = nut_ref[...]
    nt = (((1,), (1,)), ((), ()))
    heads = range(ATTN_HEADS_PER_STEP)
    step = ATTN_KEY_BLOCKS_PER_TRIP

    def sweep(kbs, masks=None, queries=slice(None)):
        masks = masks or [None] * len(kbs)
        zs = [[lax.dot_general(k_ref[pl.ds(pl.multiple_of(kb * bk, bk), bk), hh * dh:(hh + 1) * dh],
                               q_ref[queries, hh * dh:(hh + 1) * dh], nt,
                               preferred_element_type=F32) for kb in kbs] for hh in heads]
        xs, sums = [], []
        for hh in heads:
            newer = None
            blocks = []
            for n in range(len(kbs)):
                e = jnp.exp2(jnp.minimum(zs[hh][n], MAX_LOG2_SCORE))
                sp = jnp.log(1.0 + e) * LOG2E
                if masks[n] is not None:
                    sp = jnp.where(masks[n], sp, 0.0)
                tail = jnp.dot(nut, sp.astype(BF16), preferred_element_type=F32)
                blocks.append((e, tail if newer is None else tail + newer))
                newer = tail[0:1, :] if newer is None else newer + tail[0:1, :]
            xs.append(blocks)
            sums.append(newer)
        v_rows = pl.ds(pl.multiple_of(kbs[-1] * bk, bk), len(kbs) * bk)
        for hh in heads:
            ws = []
            for n in reversed(range(len(kbs))):
                e, t = xs[hh][n]
                w = e * jnp.exp2(t)
                if masks[n] is not None:
                    w = jnp.where(masks[n], w, 0.0)
                ws.append(w.astype(BF16))
            vt = v_ref[v_rows, hh * dh:(hh + 1) * dh].T
            pv = jnp.dot(vt, jnp.concatenate(ws, axis=0), preferred_element_type=F32)
            acc_ref[hh, :, queries] = acc_ref[hh, :, queries] + jnp.exp2(carry_ref[hh, :, queries]) * pv
            carry_ref[hh, :, queries] = carry_ref[hh, :, queries] + sums[hh]

    acc_ref[...] = jnp.zeros(acc_ref.shape, F32)
    carry_ref[...] = jnp.zeros(carry_ref.shape, F32)

    before_query = (lax.broadcasted_iota(jnp.int32, (bk, bk), 0)
                    < lax.broadcasted_iota(jnp.int32, (bk, bk), 1))
    for r in range(step):
        sweep([step * qi + r - m for m in range(r + 1)], [before_query] + [None] * r,
              queries=slice(r * bk, (r + 1) * bk))

    def older(j, _):
        newest = step * (qi - j) - 1
        sweep([newest - n for n in range(step)])
        return 0

    lax.fori_loop(0, qi, older, 0)

    for hh in heads:
        o_ref[:, hh * dh:(hh + 1) * dh] = acc_ref[hh].T.astype(o_ref.dtype)


def _attention(qkv, n_heads, bq=512, bk=256):
    s, w3 = qkv.shape
    dh = w3 // (3 * n_heads)
    hps = ATTN_HEADS_PER_STEP
    groups = n_heads // hps
    assert bq == ATTN_KEY_BLOCKS_PER_TRIP * bk
    nut = -(jnp.arange(bk)[None, :] >= jnp.arange(bk)[:, None]).astype(BF16)
    return pl.pallas_call(
        functools.partial(_attn_kernel, bq=bq, bk=bk, dh=dh),
        grid=(groups, s // bq),
        in_specs=[
            pl.BlockSpec((bq, hps * dh), lambda g, i: (i, g)),
            pl.BlockSpec((s, hps * dh), lambda g, i: (0, groups + g)),
            pl.BlockSpec((s, hps * dh), lambda g, i: (0, 2 * groups + g)),
            pl.BlockSpec((bk, bk), lambda g, i: (0, 0)),
        ],
        out_specs=pl.BlockSpec((bq, hps * dh), lambda g, i: (i, g)),
        out_shape=jax.ShapeDtypeStruct((s, n_heads * dh), BF16),
        scratch_shapes=[pltpu.VMEM((hps, dh, bq), F32), pltpu.VMEM((hps, 1, bq), F32)],
        compiler_params=_params(("parallel", "arbitrary")),
        name="sb_attention",
    )(qkv, qkv, qkv, nut)


def kernel(x, p, norm_mix_g, norm_mlp_g, norm_ple_g, norm_f_g, w_in_rec, conv_a_w, conv_a_b, ln_a_g, ln_a_b, conv_b_w, conv_b_b, w_rg_a, b_rg_a, w_rg_x, b_rg_x, rg_lambda, w_out_rec, w_qkv, w_o_attn, w_mlp_up, w_mlp_down, w_ple_proj, w_ple_gate):
    bsz, seq, d = x.shape
    depth = p.shape[0]
    outs = []
    w_in_rec, w_out_rec, w_qkv, w_o_attn, w_ple_proj, w_ple_gate, w_rg_a, w_rg_x = (
        w.astype(BF16) for w in (w_in_rec, w_out_rec, w_qkv, w_o_attn,
                                 w_ple_proj, w_ple_gate, w_rg_a, w_rg_x))
    q_scale = LOG2E / math.sqrt(d // SB_HEADS)
    qkv_scale = jnp.concatenate([jnp.full((d,), q_scale, F32), jnp.ones((2 * d,), F32)])
    for b in range(bsz):
        h = x[b]
        for i in range(depth):
            j = i // 2
            if i % 2 == 0:
                u = _norm_matmul(h, norm_mix_g[i], w_in_rec, j, F32)
                y = _recmix(u, conv_a_w[j], conv_a_b[j], ln_a_g[j], ln_a_b[j], conv_b_w[j], conv_b_b[j],
                            w_rg_a, b_rg_a[j], w_rg_x, b_rg_x[j], rg_lambda[j], j)
                h = _matmul_residual(y, w_out_rec, j, h)
            else:
                qkv = _norm_matmul(h, norm_mix_g[i], w_qkv, j, BF16, col_scale=qkv_scale)
                o = _attention(qkv, SB_HEADS)
                h = _matmul_residual(o, w_o_attn, j, h)
            h = _mlp(h, norm_mlp_g[i], w_mlp_up, w_mlp_down, i)
            h = _ple(h, norm_ple_g[i], w_ple_gate, p, w_ple_proj, i, b, norm_f_g,
                     final_norm=(i == depth - 1))
        outs.append(h)
    return jnp.stack(outs, axis=0)
```

```python
import functools
import math

import jax
import jax.numpy as jnp
from jax import lax
from jax.experimental import pallas as pl
from jax.experimental.pallas import tpu as pltpu

F32 = jnp.float32
BF16 = jnp.bfloat16

EPS = 1e-6
RG_C = 8.0
RG_HEADS = 8
SB_HEADS = 16
CONV_A_HALO = 32
CONV_B_HALO = 8
LANES = 128
SUBLANES = 8
VMEM_LIMIT = 56 * 1024 * 1024


def _params(semantics):
    return pltpu.CompilerParams(dimension_semantics=semantics, vmem_limit_bytes=VMEM_LIMIT)


def _rms_normed(x, g):
    ms = jnp.mean(x * x, axis=-1, keepdims=True)
    return x * lax.rsqrt(ms + EPS) * g


def _sigmoid(x):
    return 1.0 / (1.0 + jnp.exp(-x))


def _softplus(x):
    return jnp.maximum(x, 0.0) + jnp.log(1.0 + jnp.exp(-jnp.abs(x)))


def _norm_matmul_kernel(x_ref, g_ref, w_ref, *rest, scaled):
    cs_ref = rest[0] if scaled else None
    o_ref, xn_ref = rest[-2:]

    @pl.when(pl.program_id(1) == 0)
    def _():
        xn_ref[...] = _rms_normed(x_ref[...], g_ref[...]).astype(BF16)

    acc = jnp.dot(xn_ref[...], w_ref[...], preferred_element_type=F32)
    if scaled:
        acc = acc * cs_ref[...]
    o_ref[...] = acc.astype(o_ref.dtype)


def _norm_matmul(x, g, w, layer, out_dtype, col_scale=None, tm=1024, tn=1024):
    m, d = x.shape
    n = w.shape[2]
    scaled = col_scale is not None
    in_specs = [
        pl.BlockSpec((tm, d), lambda i, j: (i, 0)),
        pl.BlockSpec((1, d), lambda i, j: (0, 0)),
        pl.BlockSpec((None, d, tn), lambda i, j: (layer, 0, j)),
    ]
    args = [x, g.reshape(1, d), w]
    if scaled:
        in_specs.append(pl.BlockSpec((1, tn), lambda i, j: (0, j)))
        args.append(col_scale.reshape(1, n))
    return pl.pallas_call(
        functools.partial(_norm_matmul_kernel, scaled=scaled),
        grid=(m // tm, n // tn),
        in_specs=in_specs,
        out_specs=pl.BlockSpec((tm, tn), lambda i, j: (i, j)),
        out_shape=jax.ShapeDtypeStruct((m, n), out_dtype),
        scratch_shapes=[pltpu.VMEM((tm, d), BF16)],
        compiler_params=_params(("parallel", "arbitrary")),
        name="norm_matmul",
    )(*args)


def _matmul_residual_kernel(y_ref, w_ref, h_ref, o_ref):
    o_ref[...] = h_ref[...] + jnp.dot(y_ref[...], w_ref[...], preferred_element_type=F32)


def _matmul_residual(y, w, layer, h, tm=512):
    m, k = y.shape
    n = w.shape[2]
    return pl.pallas_call(
        _matmul_residual_kernel,
        grid=(m // tm,),
        in_specs=[
            pl.BlockSpec((tm, k), lambda i: (i, 0)),
            pl.BlockSpec((None, k, n), lambda i: (layer, 0, 0)),
            pl.BlockSpec((tm, n), lambda i: (i, 0)),
        ],
        out_specs=pl.BlockSpec((tm, n), lambda i: (i, 0)),
        out_shape=jax.ShapeDtypeStruct((m, n), F32),
        compiler_params=_params(("parallel",)),
        name="matmul_residual",
    )(y, w, h)


def _mlp_kernel(h_ref, g_ref, wu_ref, wd_ref, o_ref, xn_ref):
    @pl.when(pl.program_id(1) == 0)
    def _():
        h = h_ref[...]
        xn_ref[...] = _rms_normed(h, g_ref[...]).astype(BF16)
        o_ref[...] = h

    a = jnp.dot(xn_ref[...], wu_ref[...].astype(BF16), preferred_element_type=F32)
    a = jnp.maximum(a, 0.0)
    a = (a * a).astype(BF16)
    o_ref[...] += jnp.dot(a, wd_ref[...].astype(BF16), preferred_element_type=F32)


def _mlp(h, g, w_up, w_down, layer, tm=1024, tf=512):
    m, d = h.shape
    f = w_up.shape[2]
    return pl.pallas_call(
        _mlp_kernel,
        grid=(m // tm, f // tf),
        in_specs=[
            pl.BlockSpec((tm, d), lambda i, j: (i, 0)),
            pl.BlockSpec((1, d), lambda i, j: (0, 0)),
            pl.BlockSpec((None, d, tf), lambda i, j: (layer, 0, j)),
            pl.BlockSpec((None, tf, d), lambda i, j: (layer, j, 0)),
        ],
        out_specs=pl.BlockSpec((tm, d), lambda i, j: (i, 0)),
        out_shape=jax.ShapeDtypeStruct((m, d), F32),
        scratch_shapes=[pltpu.VMEM((tm, d), BF16)],
        compiler_params=_params(("parallel", "arbitrary")),
        name="mlp",
    )(h, g.reshape(1, d), w_up, w_down)


def _ple_kernel(h_ref, g_ref, wg_ref, p_ref, wp_ref, gf_ref, o_ref, *, final_norm):
    h = h_ref[...]
    xn = _rms_normed(h, g_ref[...]).astype(BF16)
    gate = _sigmoid(jnp.dot(xn, wg_ref[...], preferred_element_type=F32))
    emb = jnp.dot(p_ref[...].astype(BF16), wp_ref[...], preferred_element_type=F32)
    out = h + emb * gate
    if final_norm:
        out = _rms_normed(out, gf_ref[...])
    o_ref[...] = out


def _ple(h, g, w_gate, p, w_proj, layer, batch, g_final, final_norm, tm=256):
    m, d = h.shape
    pd = p.shape[3]
    return pl.pallas_call(
        functools.partial(_ple_kernel, final_norm=final_norm),
        grid=(m // tm,),
        in_specs=[
            pl.BlockSpec((tm, d), lambda i: (i, 0)),
            pl.BlockSpec((1, d), lambda i: (0, 0)),
            pl.BlockSpec((None, d, d), lambda i: (layer, 0, 0)),
            pl.BlockSpec((None, None, tm, pd), lambda i: (layer, batch, i, 0)),
            pl.BlockSpec((None, pd, d), lambda i: (layer, 0, 0)),
            pl.BlockSpec((1, d), lambda i: (0, 0)),
        ],
        out_specs=pl.BlockSpec((tm, d), lambda i: (i, 0)),
        out_shape=jax.ShapeDtypeStruct((m, d), F32),
        compiler_params=_params(("parallel",)),
        name="ple",
    )(h, g.reshape(1, d), w_gate, p, w_proj, g_final.reshape(1, d))


def _shift_rows(x, d, fill):
    rolled = pltpu.roll(x, d, 0)
    row = lax.broadcasted_iota(jnp.int32, x.shape, 0)
    return jnp.where(row >= d, rolled, fill)


def _recmix_kernel(u_ref, caw_ref, cab_ref, lng_ref, lnb_ref, cbw_ref, cbb_ref,
                   wa_ref, ba_ref, wx_ref, bx_ref, lam_ref, o_ref,
                   glu_buf, conv_buf, xr_buf, hc_ref, *, ts, ch):
    n_chunks = ch // LANES
    ka = caw_ref.shape[0]
    kb = cbw_ref.shape[0]

    @pl.when(pl.program_id(0) == 0)
    def _():
        glu_buf[0:CONV_A_HALO, :] = jnp.zeros((CONV_A_HALO, ch), F32)
        xr_buf[0:CONV_B_HALO, :] = jnp.zeros((CONV_B_HALO, ch), F32)
        hc_ref[...] = jnp.zeros(hc_ref.shape, F32)

    glu_buf[CONV_A_HALO:CONV_A_HALO + ts, :] = u_ref[:, 0:ch] * _sigmoid(u_ref[:, ch:2 * ch])
    offs = [CONV_A_HALO - (ka - 1) + k for k in range(ka)]
    rows = ts + CONV_A_HALO
    for c in range(n_chunks):
        cs = slice(c * LANES, (c + 1) * LANES)
        hist = glu_buf[:, cs]
        acc = jnp.broadcast_to(cab_ref[:, cs], (ts, LANES))
        for r in range(SUBLANES):
            taps = [k for k in range(ka) if offs[k] % SUBLANES == r]
            if not taps:
                continue
            shifted = hist if r == 0 else pltpu.roll(hist, rows - r, 0)
            for k in taps:
                acc = acc + caw_ref[k:k + 1, cs] * shifted[offs[k] - r:offs[k] - r + ts, :]
        conv_buf[:, cs] = acc
    glu_buf[0:CONV_A_HALO, :] = glu_buf[ts:ts + CONV_A_HALO, :]

    ya = conv_buf[...]
    mu = jnp.mean(ya, axis=-1, keepdims=True)
    yc = ya - mu
    var = jnp.mean(yc * yc, axis=-1, keepdims=True)
    yn = yc * lax.rsqrt(var + EPS) * lng_ref[...] + lnb_ref[...]
    o_ref[:, 0:ch] = (yn * _sigmoid(yn)).astype(o_ref.dtype)

    xr_buf[CONV_B_HALO:CONV_B_HALO + ts, :] = u_ref[:, 2 * ch:3 * ch]
    for c in range(n_chunks):
        cs = slice(c * LANES, (c + 1) * LANES)
        xc = jnp.broadcast_to(cbb_ref[:, cs], (ts, LANES))
        for k in range(kb):
            off = CONV_B_HALO - (kb - 1) + k
            xc = xc + cbw_ref[k:k + 1, cs] * xr_buf[off:off + ts, cs]
        xcb = xc.astype(BF16)
        r = _sigmoid(jnp.dot(xcb, wa_ref[c], preferred_element_type=F32) + ba_ref[:, cs])
        gi = _sigmoid(jnp.dot(xcb, wx_ref[c], preferred_element_type=F32) + bx_ref[:, cs])
        log_a = (-RG_C) * r * _softplus(-lam_ref[:, cs])
        a = jnp.exp(log_a)
        uu = jnp.sqrt(1.0 - jnp.exp(2.0 * log_a)) * (gi * xc)
        d = 1
        while d < ts:
            a_prev = _shift_rows(a, d, 1.0)
            u_prev = _shift_rows(uu, d, 0.0)
            uu = a * u_prev + uu
            a = a * a_prev
            d *= 2
        h = uu + a * hc_ref[:, cs]
        hc_ref[:, cs] = h[ts - 1:ts, :]
        gr = u_ref[:, 3 * ch + c * LANES:3 * ch + (c + 1) * LANES]
        gelu = 0.5 * gr * (1.0 + jnp.tanh(math.sqrt(2.0 / math.pi) * (gr + 0.044715 * (gr * gr * gr))))
        o_ref[:, ch + c * LANES:ch + (c + 1) * LANES] = (h * gelu).astype(o_ref.dtype)
    xr_buf[0:CONV_B_HALO, :] = xr_buf[ts:ts + CONV_B_HALO, :]


def _recmix(u, caw, cab, lng, lnb, cbw, cbb, wa, ba, wx, bx, lam, layer, ts=256):
    s, w4 = u.shape
    ch = w4 // 4
    row = lambda v: v.reshape(1, ch)
    const2 = lambda i: (0, 0)
    gate_spec = pl.BlockSpec((None,) + wa.shape[1:], lambda i: (layer, 0, 0, 0))
    return pl.pallas_call(
        functools.partial(_recmix_kernel, ts=ts, ch=ch),
        grid=(s // ts,),
        in_specs=[
            pl.BlockSpec((ts, w4), lambda i: (i, 0)),
            pl.BlockSpec(caw.shape, const2),
            pl.BlockSpec((1, ch), const2),
            pl.BlockSpec((1, ch), const2),
            pl.BlockSpec((1, ch), const2),
            pl.BlockSpec(cbw.shape, const2),
            pl.BlockSpec((1, ch), const2),
            gate_spec,
            pl.BlockSpec((1, ch), const2),
            gate_spec,
            pl.BlockSpec((1, ch), const2),
            pl.BlockSpec((1, ch), const2),
        ],
        out_specs=pl.BlockSpec((ts, 2 * ch), lambda i: (i, 0)),
        out_shape=jax.ShapeDtypeStruct((s, 2 * ch), BF16),
        scratch_shapes=[
            pltpu.VMEM((ts + CONV_A_HALO, ch), F32),
            pltpu.VMEM((ts, ch), F32),
            pltpu.VMEM((ts + CONV_B_HALO, ch), F32),
            pltpu.VMEM((1, ch), F32),
        ],
        compiler_params=_params(("arbitrary",)),
        name="recmix",
    )(u, caw, row(cab), row(lng), row(lnb), cbw, row(cbb), wa, row(ba), wx, row(bx), row(lam))


ATTN_HEADS_PER_STEP = 4
ATTN_KEY_BLOCKS_PER_TRIP = 2
LOG2E = 1.4426950408889634
MAX_LOG2_SCORE = 126.0


def _attn_kernel(q_ref, k_ref, v_ref, ntri_ref, o_ref, acc_ref, carry_ref, *, bq, bk, dh):
    qi = pl.program_id(1)
    ntri = ntri_ref[...]
    nt = (((1,), (1,)), ((), ()))
    heads = range(ATTN_HEADS_PER_STEP)
    step = ATTN_KEY_BLOCKS_PER_TRIP

    def block_ref(ref, hh, kb):
        return ref[pl.ds(pl.multiple_of(kb * bk, bk), bk), hh * dh:(hh + 1) * dh]

    def sweep(kbs, masks=None, rows=slice(None)):
        masks = masks or [None] * len(kbs)
        zs = [[lax.dot_general(q_ref[rows, hh * dh:(hh + 1) * dh], block_ref(k_ref, hh, kb), nt,
                               preferred_element_type=F32) for kb in kbs] for hh in heads]
        xs, sums = [], []
        for hh in heads:
            newer = None
            row = []
            for n in range(len(kbs)):
                e = jnp.exp2(jnp.minimum(zs[hh][n], MAX_LOG2_SCORE))
                sp = jnp.log(1.0 + e) * LOG2E
                if masks[n] is not None:
                    sp = jnp.where(masks[n], sp, 0.0)
                tail = jnp.dot(sp.astype(BF16), ntri, preferred_element_type=F32)
                row.append((e, tail if newer is None else tail + newer))
                newer = tail[:, 0:1] if newer is None else newer + tail[:, 0:1]
            xs.append(row)
            sums.append(newer)
        v_rows = pl.ds(pl.multiple_of(kbs[-1] * bk, bk), len(kbs) * bk)
        for hh in heads:
            ws = []
            for n in reversed(range(len(kbs))):
                e, t = xs[hh][n]
                w = e * jnp.exp2(t)
                if masks[n] is not None:
                    w = jnp.where(masks[n], w, 0.0)
                ws.append(w.astype(BF16))
            pv = jnp.dot(jnp.concatenate(ws, axis=1), v_ref[v_rows, hh * dh:(hh + 1) * dh],
                         preferred_element_type=F32)
            acc_ref[hh, rows] = acc_ref[hh, rows] + jnp.exp2(carry_ref[hh, rows]) * pv
            carry_ref[hh, rows] = carry_ref[hh, rows] + sums[hh]

    acc_ref[...] = jnp.zeros(acc_ref.shape, F32)
    carry_ref[...] = jnp.zeros(carry_ref.shape, F32)

    below_diagonal = (lax.broadcasted_iota(jnp.int32, (bk, bk), 1)
                      < lax.broadcasted_iota(jnp.int32, (bk, bk), 0))
    for r in range(step):
        sweep([step * qi + r - m for m in range(r + 1)], [below_diagonal] + [None] * r,
              rows=slice(r * bk, (r + 1) * bk))

    def older(j, _):
        newest = step * (qi - j) - 1
        sweep([newest - n for n in range(step)])
        return 0

    lax.fori_loop(0, qi, older, 0)

    for hh in heads:
        o_ref[:, hh * dh:(hh + 1) * dh] = acc_ref[hh].astype(o_ref.dtype)


def _attention(qkv, n_heads, bq=512, bk=256):
    s, w3 = qkv.shape
    dh = w3 // (3 * n_heads)
    hps = ATTN_HEADS_PER_STEP
    groups = n_heads // hps
    assert bq == ATTN_KEY_BLOCKS_PER_TRIP * bk
    tri = (jnp.arange(bk)[:, None] >= jnp.arange(bk)[None, :])
    ntri = -tri.astype(BF16)
    return pl.pallas_call(
        functools.partial(_attn_kernel, bq=bq, bk=bk, dh=dh),
        grid=(groups, s // bq),
        in_specs=[
            pl.BlockSpec((bq, hps * dh), lambda g, i: (i, g)),
            pl.BlockSpec((s, hps * dh), lambda g, i: (0, groups + g)),
            pl.BlockSpec((s, hps * dh), lambda g, i: (0, 2 * groups + g)),
            pl.BlockSpec((bk, bk), lambda g, i: (0, 0)),
        ],
        out_specs=pl.BlockSpec((bq, hps * dh), lambda g, i: (i, g)),
        out_shape=jax.ShapeDtypeStruct((s, n_heads * dh), BF16),
        scratch_shapes=[pltpu.VMEM((hps, bq, dh), F32), pltpu.VMEM((hps, bq, 1), F32)],
        compiler_params=_params(("parallel", "arbitrary")),
        name="sb_attention",
    )(qkv, qkv, qkv, ntri)


def kernel(x, p, norm_mix_g, norm_mlp_g, norm_ple_g, norm_f_g, w_in_rec, conv_a_w, conv_a_b, ln_a_g, ln_a_b, conv_b_w, conv_b_b, w_rg_a, b_rg_a, w_rg_x, b_rg_x, rg_lambda, w_out_rec, w_qkv, w_o_attn, w_mlp_up, w_mlp_down, w_ple_proj, w_ple_gate):
    bsz, seq, d = x.shape
    depth = p.shape[0]
    outs = []
    w_in_rec, w_out_rec, w_qkv, w_o_attn, w_ple_proj, w_ple_gate, w_rg_a, w_rg_x = (
        w.astype(BF16) for w in (w_in_rec, w_out_rec, w_qkv, w_o_attn,
                                 w_ple_proj, w_ple_gate, w_rg_a, w_rg_x))
    q_scale = LOG2E / math.sqrt(d // SB_HEADS)
    qkv_scale = jnp.concatenate([jnp.full((d,), q_scale, F32), jnp.ones((2 * d,), F32)])
    for b in range(bsz):
        h = x[b]
        for i in range(depth):
            j = i // 2
            if i % 2 == 0:
                u = _norm_matmul(h, norm_mix_g[i], w_in_rec, j, F32)
                y = _recmix(u, conv_a_w[j], conv_a_b[j], ln_a_g[j], ln_a_b[j], conv_b_w[j], conv_b_b[j],
                            w_rg_a, b_rg_a[j], w_rg_x, b_rg_x[j], rg_lambda[j], j)
                h = _matmul_residual(y, w_out_rec, j, h)
            else:
                qkv = _norm_matmul(h, norm_mix_g[i], w_qkv, j, BF16, col_scale=qkv_scale)
                o = _attention(qkv, SB_HEADS)
                h = _matmul_residual(o, w_o_attn, j, h)
            h = _mlp(h, norm_mlp_g[i], w_mlp_up, w_mlp_down, i)
            h = _ple(h, norm_ple_g[i], w_ple_gate, p, w_ple_proj, i, b, norm_f_g,
                     final_norm=(i == depth - 1))
        outs.append(h)
    return jnp.stack(outs, axis=0)
```

```python
import functools
import math

import jax
import jax.numpy as jnp
from jax import lax
from jax.experimental import pallas as pl
from jax.experimental.pallas import tpu as pltpu

F32 = jnp.float32
BF16 = jnp.bfloat16

EPS = 1e-6
RG_C = 8.0
RG_HEADS = 8
SB_HEADS = 16
CONV_A_HALO = 32
CONV_B_HALO = 8
LANES = 128
SUBLANES = 8
VMEM_LIMIT = 56 * 1024 * 1024


def _params(semantics):
    return pltpu.CompilerParams(dimension_semantics=semantics, vmem_limit_bytes=VMEM_LIMIT)


def _rms_normed(x, g):
    ms = jnp.mean(x * x, axis=-1, keepdims=True)
    return x * lax.rsqrt(ms + EPS) * g


def _sigmoid(x):
    return 1.0 / (1.0 + jnp.exp(-x))


def _softplus(x):
    return jnp.maximum(x, 0.0) + jnp.log(1.0 + jnp.exp(-jnp.abs(x)))


def _norm_matmul_kernel(x_ref, g_ref, w_ref, *rest, scaled):
    cs_ref = rest[0] if scaled else None
    o_ref, xn_ref = rest[-2:]

    @pl.when(pl.program_id(1) == 0)
    def _():
        xn_ref[...] = _rms_normed(x_ref[...], g_ref[...]).astype(BF16)

    acc = jnp.dot(xn_ref[...], w_ref[...], preferred_element_type=F32)
    if scaled:
        acc = acc * cs_ref[...]
    o_ref[...] = acc.astype(o_ref.dtype)


def _norm_matmul(x, g, w, layer, out_dtype, col_scale=None, tm=1024, tn=1024):
    m, d = x.shape
    n = w.shape[2]
    scaled = col_scale is not None
    in_specs = [
        pl.BlockSpec((tm, d), lambda i, j: (i, 0)),
        pl.BlockSpec((1, d), lambda i, j: (0, 0)),
        pl.BlockSpec((None, d, tn), lambda i, j: (layer, 0, j)),
    ]
    args = [x, g.reshape(1, d), w]
    if scaled:
        in_specs.append(pl.BlockSpec((1, tn), lambda i, j: (0, j)))
        args.append(col_scale.reshape(1, n))
    return pl.pallas_call(
        functools.partial(_norm_matmul_kernel, scaled=scaled),
        grid=(m // tm, n // tn),
        in_specs=in_specs,
        out_specs=pl.BlockSpec((tm, tn), lambda i, j: (i, j)),
        out_shape=jax.ShapeDtypeStruct((m, n), out_dtype),
        scratch_shapes=[pltpu.VMEM((tm, d), BF16)],
        compiler_params=_params(("parallel", "arbitrary")),
        name="norm_matmul",
    )(*args)


def _matmul_residual_kernel(y_ref, w_ref, h_ref, o_ref):
    o_ref[...] = h_ref[...] + jnp.dot(y_ref[...], w_ref[...], preferred_element_type=F32)


def _matmul_residual(y, w, layer, h, tm=512):
    m, k = y.shape
    n = w.shape[2]
    return pl.pallas_call(
        _matmul_residual_kernel,
        grid=(m // tm,),
        in_specs=[
            pl.BlockSpec((tm, k), lambda i: (i, 0)),
            pl.BlockSpec((None, k, n), lambda i: (layer, 0, 0)),
            pl.BlockSpec((tm, n), lambda i: (i, 0)),
        ],
        out_specs=pl.BlockSpec((tm, n), lambda i: (i, 0)),
        out_shape=jax.ShapeDtypeStruct((m, n), F32),
        compiler_params=_params(("parallel",)),
        name="matmul_residual",
    )(y, w, h)


def _mlp_kernel(h_ref, g_ref, wu_ref, wd_ref, o_ref, xn_ref):
    @pl.when(pl.program_id(1) == 0)
    def _():
        h = h_ref[...]
        xn_ref[...] = _rms_normed(h, g_ref[...]).astype(BF16)
        o_ref[...] = h

    a = jnp.dot(xn_ref[...], wu_ref[...].astype(BF16), preferred_element_type=F32)
    a = jnp.maximum(a, 0.0)
    a = (a * a).astype(BF16)
    o_ref[...] += jnp.dot(a, wd_ref[...].astype(BF16), preferred_element_type=F32)


def _mlp(h, g, w_up, w_down, layer, tm=1024, tf=512):
    m, d = h.shape
    f = w_up.shape[2]
    return pl.pallas_call(
        _mlp_kernel,
        grid=(m // tm, f // tf),
        in_specs=[
            pl.BlockSpec((tm, d), lambda i, j: (i, 0)),
            pl.BlockSpec((1, d), lambda i, j: (0, 0)),
            pl.BlockSpec((None, d, tf), lambda i, j: (layer, 0, j)),
            pl.BlockSpec((None, tf, d), lambda i, j: (layer, j, 0)),
        ],
        out_specs=pl.BlockSpec((tm, d), lambda i, j: (i, 0)),
        out_shape=jax.ShapeDtypeStruct((m, d), F32),
        scratch_shapes=[pltpu.VMEM((tm, d), BF16)],
        compiler_params=_params(("parallel", "arbitrary")),
        name="mlp",
    )(h, g.reshape(1, d), w_up, w_down)


def _ple_kernel(h_ref, g_ref, wg_ref, p_ref, wp_ref, gf_ref, o_ref, *, final_norm):
    h = h_ref[...]
    xn = _rms_normed(h, g_ref[...]).astype(BF16)
    gate = _sigmoid(jnp.dot(xn, wg_ref[...], preferred_element_type=F32))
    emb = jnp.dot(p_ref[...].astype(BF16), wp_ref[...], preferred_element_type=F32)
    out = h + emb * gate
    if final_norm:
        out = _rms_normed(out, gf_ref[...])
    o_ref[...] = out


def _ple(h, g, w_gate, p, w_proj, layer, batch, g_final, final_norm, tm=256):
    m, d = h.shape
    pd = p.shape[3]
    return pl.pallas_call(
        functools.partial(_ple_kernel, final_norm=final_norm),
        grid=(m // tm,),
        in_specs=[
            pl.BlockSpec((tm, d), lambda i: (i, 0)),
            pl.BlockSpec((1, d), lambda i: (0, 0)),
            pl.BlockSpec((None, d, d), lambda i: (layer, 0, 0)),
            pl.BlockSpec((None, None, tm, pd), lambda i: (layer, batch, i, 0)),
            pl.BlockSpec((None, pd, d), lambda i: (layer, 0, 0)),
            pl.BlockSpec((1, d), lambda i: (0, 0)),
        ],
        out_specs=pl.BlockSpec((tm, d), lambda i: (i, 0)),
        out_shape=jax.ShapeDtypeStruct((m, d), F32),
        compiler_params=_params(("parallel",)),
        name="ple",
    )(h, g.reshape(1, d), w_gate, p, w_proj, g_final.reshape(1, d))


def _shift_rows(x, d, fill):
    rolled = pltpu.roll(x, d, 0)
    row = lax.broadcasted_iota(jnp.int32, x.shape, 0)
    return jnp.where(row >= d, rolled, fill)


def _recmix_kernel(u_ref, caw_ref, cab_ref, lng_ref, lnb_ref, cbw_ref, cbb_ref,
                   wa_ref, ba_ref, wx_ref, bx_ref, lam_ref, o_ref,
                   glu_buf, conv_buf, xr_buf, hc_ref, *, ts, ch):
    n_chunks = ch // LANES
    ka = caw_ref.shape[0]
    kb = cbw_ref.shape[0]

    @pl.when(pl.program_id(0) == 0)
    def _():
        glu_buf[0:CONV_A_HALO, :] = jnp.zeros((CONV_A_HALO, ch), F32)
        xr_buf[0:CONV_B_HALO, :] = jnp.zeros((CONV_B_HALO, ch), F32)
        hc_ref[...] = jnp.zeros(hc_ref.shape, F32)

    glu_buf[CONV_A_HALO:CONV_A_HALO + ts, :] = u_ref[:, 0:ch] * _sigmoid(u_ref[:, ch:2 * ch])
    offs = [CONV_A_HALO - (ka - 1) + k for k in range(ka)]
    rows = ts + CONV_A_HALO
    for c in range(n_chunks):
        cs = slice(c * LANES, (c + 1) * LANES)
        hist = glu_buf[:, cs]
        acc = jnp.broadcast_to(cab_ref[:, cs], (ts, LANES))
        for r in range(SUBLANES):
            taps = [k for k in range(ka) if offs[k] % SUBLANES == r]
            if not taps:
                continue
            shifted = hist if r == 0 else pltpu.roll(hist, rows - r, 0)
            for k in taps:
                acc = acc + caw_ref[k:k + 1, cs] * shifted[offs[k] - r:offs[k] - r + ts, :]
        conv_buf[:, cs] = acc
    glu_buf[0:CONV_A_HALO, :] = glu_buf[ts:ts + CONV_A_HALO, :]

    ya = conv_buf[...]
    mu = jnp.mean(ya, axis=-1, keepdims=True)
    yc = ya - mu
    var = jnp.mean(yc * yc, axis=-1, keepdims=True)
    yn = yc * lax.rsqrt(var + EPS) * lng_ref[...] + lnb_ref[...]
    o_ref[:, 0:ch] = (yn * _sigmoid(yn)).astype(o_ref.dtype)

    xr_buf[CONV_B_HALO:CONV_B_HALO + ts, :] = u_ref[:, 2 * ch:3 * ch]
    for c in range(n_chunks):
        cs = slice(c * LANES, (c + 1) * LANES)
        xc = jnp.broadcast_to(cbb_ref[:, cs], (ts, LANES))
        for k in range(kb):
            off = CONV_B_HALO - (kb - 1) + k
            xc = xc + cbw_ref[k:k + 1, cs] * xr_buf[off:off + ts, cs]
        xcb = xc.astype(BF16)
        r = _sigmoid(jnp.dot(xcb, wa_ref[c], preferred_element_type=F32) + ba_ref[:, cs])
        gi = _sigmoid(jnp.dot(xcb, wx_ref[c], preferred_element_type=F32) + bx_ref[:, cs])
        log_a = (-RG_C) * r * _softplus(-lam_ref[:, cs])
        a = jnp.exp(log_a)
        uu = jnp.sqrt(1.0 - jnp.exp(2.0 * log_a)) * (gi * xc)
        d = 1
        while d < ts:
            a_prev = _shift_rows(a, d, 1.0)
            u_prev = _shift_rows(uu, d, 0.0)
            uu = a * u_prev + uu
            a = a * a_prev
            d *= 2
        h = uu + a * hc_ref[:, cs]
        hc_ref[:, cs] = h[ts - 1:ts, :]
        gr = u_ref[:, 3 * ch + c * LANES:3 * ch + (c + 1) * LANES]
        gelu = 0.5 * gr * (1.0 + jnp.tanh(math.sqrt(2.0 / math.pi) * (gr + 0.044715 * (gr * gr * gr))))
        o_ref[:, ch + c * LANES:ch + (c + 1) * LANES] = (h * gelu).astype(o_ref.dtype)
    xr_buf[0:CONV_B_HALO, :] = xr_buf[ts:ts + CONV_B_HALO, :]


def _recmix(u, caw, cab, lng, lnb, cbw, cbb, wa, ba, wx, bx, lam, layer, ts=256):
    s, w4 = u.shape
    ch = w4 // 4
    row = lambda v: v.reshape(1, ch)
    const2 = lambda i: (0, 0)
    gate_spec = pl.BlockSpec((None,) + wa.shape[1:], lambda i: (layer, 0, 0, 0))
    return pl.pallas_call(
        functools.partial(_recmix_kernel, ts=ts, ch=ch),
        grid=(s // ts,),
        in_specs=[
            pl.BlockSpec((ts, w4), lambda i: (i, 0)),
            pl.BlockSpec(caw.shape, const2),
            pl.BlockSpec((1, ch), const2),
            pl.BlockSpec((1, ch), const2),
            pl.BlockSpec((1, ch), const2),
            pl.BlockSpec(cbw.shape, const2),
            pl.BlockSpec((1, ch), const2),
            gate_spec,
            pl.BlockSpec((1, ch), const2),
            gate_spec,
            pl.BlockSpec((1, ch), const2),
            pl.BlockSpec((1, ch), const2),
        ],
        out_specs=pl.BlockSpec((ts, 2 * ch), lambda i: (i, 0)),
        out_shape=jax.ShapeDtypeStruct((s, 2 * ch), BF16),
        scratch_shapes=[
            pltpu.VMEM((ts + CONV_A_HALO, ch), F32),
            pltpu.VMEM((ts, ch), F32),
            pltpu.VMEM((ts + CONV_B_HALO, ch), F32),
            pltpu.VMEM((1, ch), F32),
        ],
        compiler_params=_params(("arbitrary",)),
        name="recmix",
    )(u, caw, row(cab), row(lng), row(lnb), cbw, row(cbb), wa, row(ba), wx, row(bx), row(lam))


ATTN_HEADS_PER_STEP = 4
ATTN_KEY_BLOCKS_PER_TRIP = 2
LOG2E = 1.4426950408889634
MAX_LOG2_SCORE = 126.0


def _attn_kernel(q_ref, k_ref, vt_ref, nut_ref, o_ref, acc_ref, carry_ref, *, bq, bk, dh):
    qi = pl.program_id(1)
    nut = nut_ref[...]
    nt = (((1,), (1,)), ((), ()))
    heads = range(ATTN_HEADS_PER_STEP)
    step = ATTN_KEY_BLOCKS_PER_TRIP

    def sweep(kbs, masks=None, queries=slice(None)):
        masks = masks or [None] * len(kbs)
        zs = [[lax.dot_general(k_ref[pl.ds(pl.multiple_of(kb * bk, bk), bk), hh * dh:(hh + 1) * dh],
                               q_ref[queries, hh * dh:(hh + 1) * dh], nt,
                               preferred_element_type=F32) for kb in kbs] for hh in heads]
        xs, sums = [], []
        for hh in heads:
            newer = None
            blocks = []
            for n in range(len(kbs)):
                e = jnp.exp2(jnp.minimum(zs[hh][n], MAX_LOG2_SCORE))
                sp = jnp.log(1.0 + e) * LOG2E
                if masks[n] is not None:
                    sp = jnp.where(masks[n], sp, 0.0)
                tail = jnp.dot(nut, sp.astype(BF16), preferred_element_type=F32)
                blocks.append((e, tail if newer is None else tail + newer))
                newer = tail[0:1, :] if newer is None else newer + tail[0:1, :]
            xs.append(blocks)
            sums.append(newer)
        for hh in heads:
            ws = []
            for n in reversed(range(len(kbs))):
                e, t = xs[hh][n]
                w = e * jnp.exp2(t)
                if masks[n] is not None:
                    w = jnp.where(masks[n], w, 0.0)
                ws.append(w.astype(BF16))
            vt = jnp.concatenate([vt_ref[kbs[-1] + m, hh * dh:(hh + 1) * dh, :] for m in range(len(kbs))],
                                 axis=1)
            pv = jnp.dot(vt, jnp.concatenate(ws, axis=0), preferred_element_type=F32)
            acc_ref[hh, :, queries] = acc_ref[hh, :, queries] + jnp.exp2(carry_ref[hh, :, queries]) * pv
            carry_ref[hh, :, queries] = carry_ref[hh, :, queries] + sums[hh]

    acc_ref[...] = jnp.zeros(acc_ref.shape, F32)
    carry_ref[...] = jnp.zeros(carry_ref.shape, F32)

    before_query = (lax.broadcasted_iota(jnp.int32, (bk, bk), 0)
                    < lax.broadcasted_iota(jnp.int32, (bk, bk), 1))
    for r in range(step):
        sweep([step * qi + r - m for m in range(r + 1)], [before_query] + [None] * r,
              queries=slice(r * bk, (r + 1) * bk))

    def older(j, _):
        newest = step * (qi - j) - 1
        sweep([newest - n for n in range(step)])
        return 0

    lax.fori_loop(0, qi, older, 0)

    for hh in heads:
        o_ref[:, hh * dh:(hh + 1) * dh] = acc_ref[hh].T.astype(o_ref.dtype)


def _attention(qkv, n_heads, bq=512, bk=256):
    s, w3 = qkv.shape
    dh = w3 // (3 * n_heads)
    vt = qkv[:, 2 * n_heads * dh:].reshape(s // bk, bk, n_heads * dh).transpose(0, 2, 1)
    hps = ATTN_HEADS_PER_STEP
    groups = n_heads // hps
    assert bq == ATTN_KEY_BLOCKS_PER_TRIP * bk
    nut = -(jnp.arange(bk)[None, :] >= jnp.arange(bk)[:, None]).astype(BF16)
    return pl.pallas_call(
        functools.partial(_attn_kernel, bq=bq, bk=bk, dh=dh),
        grid=(groups, s // bq),
        in_specs=[
            pl.BlockSpec((bq, hps * dh), lambda g, i: (i, g)),
            pl.BlockSpec((s, hps * dh), lambda g, i: (0, groups + g)),
            pl.BlockSpec((s // bk, hps * dh, bk), lambda g, i: (0, g, 0)),
            pl.BlockSpec((bk, bk), lambda g, i: (0, 0)),
        ],
        out_specs=pl.BlockSpec((bq, hps * dh), lambda g, i: (i, g)),
        out_shape=jax.ShapeDtypeStruct((s, n_heads * dh), BF16),
        scratch_shapes=[pltpu.VMEM((hps, dh, bq), F32), pltpu.VMEM((hps, 1, bq), F32)],
        compiler_params=_params(("parallel", "arbitrary")),
        name="sb_attention",
    )(qkv, qkv, vt, nut)


def kernel(x, p, norm_mix_g, norm_mlp_g, norm_ple_g, norm_f_g, w_in_rec, conv_a_w, conv_a_b, ln_a_g, ln_a_b, conv_b_w, conv_b_b, w_rg_a, b_rg_a, w_rg_x, b_rg_x, rg_lambda, w_out_rec, w_qkv, w_o_attn, w_mlp_up, w_mlp_down, w_ple_proj, w_ple_gate):
    bsz, seq, d = x.shape
    depth = p.shape[0]
    outs = []
    w_in_rec, w_out_rec, w_qkv, w_o_attn, w_ple_proj, w_ple_gate, w_rg_a, w_rg_x = (
        w.astype(BF16) for w in (w_in_rec, w_out_rec, w_qkv, w_o_attn,
                                 w_ple_proj, w_ple_gate, w_rg_a, w_rg_x))
    q_scale = LOG2E / math.sqrt(d // SB_HEADS)
    qkv_scale = jnp.concatenate([jnp.full((d,), q_scale, F32), jnp.ones((2 * d,), F32)])
    for b in range(bsz):
        h = x[b]
        for i in range(depth):
            j = i // 2
            if i % 2 == 0:
                u = _norm_matmul(h, norm_mix_g[i], w_in_rec, j, F32)
                y = _recmix(u, conv_a_w[j], conv_a_b[j], ln_a_g[j], ln_a_b[j], conv_b_w[j], conv_b_b[j],
                            w_rg_a, b_rg_a[j], w_rg_x, b_rg_x[j], rg_lambda[j], j)
                h = _matmul_residual(y, w_out_rec, j, h)
            else:
                qkv = _norm_matmul(h, norm_mix_g[i], w_qkv, j, BF16, col_scale=qkv_scale)
                o = _attention(qkv, SB_HEADS)
                h = _matmul_residual(o, w_o_attn, j, h)
            h = _mlp(h, norm_mlp_g[i], w_mlp_up, w_mlp_down, i)
            h = _ple(h, norm_ple_g[i], w_ple_gate, p, w_ple_proj, i, b, norm_f_g,
                     final_norm=(i == depth - 1))
        outs.append(h)
    return jnp.stack(outs, axis=0)
```

```python
import functools
import math

import jax
import jax.numpy as jnp
from jax import lax
from jax.experimental import pallas as pl
from jax.experimental.pallas import tpu as pltpu

F32 = jnp.float32
BF16 = jnp.bfloat16

EPS = 1e-6
RG_C = 8.0
RG_HEADS = 8
SB_HEADS = 16
CONV_A_HALO = 32
CONV_B_HALO = 8
LANES = 128
SUBLANES = 8
VMEM_LIMIT = 56 * 1024 * 1024


def _params(semantics):
    return pltpu.CompilerParams(dimension_semantics=semantics, vmem_limit_bytes=VMEM_LIMIT)


def _rms_normed(x, g):
    ms = jnp.mean(x * x, axis=-1, keepdims=True)
    return x * lax.rsqrt(ms + EPS) * g


def _sigmoid(x):
    return 1.0 / (1.0 + jnp.exp(-x))


def _softplus(x):
    return jnp.maximum(x, 0.0) + jnp.log(1.0 + jnp.exp(-jnp.abs(x)))


def _norm_matmul_kernel(x_ref, g_ref, w_ref, *rest, scaled):
    cs_ref = rest[0] if scaled else None
    o_ref, xn_ref = rest[-2:]

    @pl.when(pl.program_id(1) == 0)
    def _():
        xn_ref[...] = _rms_normed(x_ref[...], g_ref[...]).astype(BF16)

    acc = jnp.dot(xn_ref[...], w_ref[...], preferred_element_type=F32)
    if scaled:
        acc = acc * cs_ref[...]
    o_ref[...] = acc.astype(o_ref.dtype)


def _norm_matmul(x, g, w, layer, out_dtype, col_scale=None, tm=1024, tn=1024):
    m, d = x.shape
    n = w.shape[2]
    scaled = col_scale is not None
    in_specs = [
        pl.BlockSpec((tm, d), lambda i, j: (i, 0)),
        pl.BlockSpec((1, d), lambda i, j: (0, 0)),
        pl.BlockSpec((None, d, tn), lambda i, j: (layer, 0, j)),
    ]
    args = [x, g.reshape(1, d), w]
    if scaled:
        in_specs.append(pl.BlockSpec((1, tn), lambda i, j: (0, j)))
        args.append(col_scale.reshape(1, n))
    return pl.pallas_call(
        functools.partial(_norm_matmul_kernel, scaled=scaled),
        grid=(m // tm, n // tn),
        in_specs=in_specs,
        out_specs=pl.BlockSpec((tm, tn), lambda i, j: (i, j)),
        out_shape=jax.ShapeDtypeStruct((m, n), out_dtype),
        scratch_shapes=[pltpu.VMEM((tm, d), BF16)],
        compiler_params=_params(("parallel", "arbitrary")),
        name="norm_matmul",
    )(*args)


def _matmul_residual_kernel(y_ref, w_ref, h_ref, o_ref):
    o_ref[...] = h_ref[...] + jnp.dot(y_ref[...], w_ref[...], preferred_element_type=F32)


def _matmul_residual(y, w, layer, h, tm=512):
    m, k = y.shape
    n = w.shape[2]
    return pl.pallas_call(
        _matmul_residual_kernel,
        grid=(m // tm,),
        in_specs=[
            pl.BlockSpec((tm, k), lambda i: (i, 0)),
            pl.BlockSpec((None, k, n), lambda i: (layer, 0, 0)),
            pl.BlockSpec((tm, n), lambda i: (i, 0)),
        ],
        out_specs=pl.BlockSpec((tm, n), lambda i: (i, 0)),
        out_shape=jax.ShapeDtypeStruct((m, n), F32),
        compiler_params=_params(("parallel",)),
        name="matmul_residual",
    )(y, w, h)


def _mlp_kernel(h_ref, g_ref, wu_ref, wd_ref, o_ref, xn_ref):
    @pl.when(pl.program_id(1) == 0)
    def _():
        h = h_ref[...]
        xn_ref[...] = _rms_normed(h, g_ref[...]).astype(BF16)
        o_ref[...] = h

    a = jnp.dot(xn_ref[...], wu_ref[...].astype(BF16), preferred_element_type=F32)
    a = jnp.maximum(a, 0.0)
    a = (a * a).astype(BF16)
    o_ref[...] += jnp.dot(a, wd_ref[...].astype(BF16), preferred_element_type=F32)


def _mlp(h, g, w_up, w_down, layer, tm=1024, tf=512):
    m, d = h.shape
    f = w_up.shape[2]
    return pl.pallas_call(
        _mlp_kernel,
        grid=(m // tm, f // tf),
        in_specs=[
            pl.BlockSpec((tm, d), lambda i, j: (i, 0)),
            pl.BlockSpec((1, d), lambda i, j: (0, 0)),
            pl.BlockSpec((None, d, tf), lambda i, j: (layer, 0, j)),
            pl.BlockSpec((None, tf, d), lambda i, j: (layer, j, 0)),
        ],
        out_specs=pl.BlockSpec((tm, d), lambda i, j: (i, 0)),
        out_shape=jax.ShapeDtypeStruct((m, d), F32),
        scratch_shapes=[pltpu.VMEM((tm, d), BF16)],
        compiler_params=_params(("parallel", "arbitrary")),
        name="mlp",
    )(h, g.reshape(1, d), w_up, w_down)


def _ple_kernel(h_ref, g_ref, wg_ref, p_ref, wp_ref, gf_ref, o_ref, *, final_norm):
    h = h_ref[...]
    xn = _rms_normed(h, g_ref[...]).astype(BF16)
    gate = _sigmoid(jnp.dot(xn, wg_ref[...], preferred_element_type=F32))
    emb = jnp.dot(p_ref[...].astype(BF16), wp_ref[...], preferred_element_type=F32)
    out = h + emb * gate
    if final_norm:
        out = _rms_normed(out, gf_ref[...])
    o_ref[...] = out


def _ple(h, g, w_gate, p, w_proj, layer, batch, g_final, final_norm, tm=256):
    m, d = h.shape
    pd = p.shape[3]
    return pl.pallas_call(
        functools.partial(_ple_kernel, final_norm=final_norm),
        grid=(m // tm,),
        in_specs=[
            pl.BlockSpec((tm, d), lambda i: (i, 0)),
            pl.BlockSpec((1, d), lambda i: (0, 0)),
            pl.BlockSpec((None, d, d), lambda i: (layer, 0, 0)),
            pl.BlockSpec((None, None, tm, pd), lambda i: (layer, batch, i, 0)),
            pl.BlockSpec((None, pd, d), lambda i: (layer, 0, 0)),
            pl.BlockSpec((1, d), lambda i: (0, 0)),
        ],
        out_specs=pl.BlockSpec((tm, d), lambda i: (i, 0)),
        out_shape=jax.ShapeDtypeStruct((m, d), F32),
        compiler_params=_params(("parallel",)),
        name="ple",
    )(h, g.reshape(1, d), w_gate, p, w_proj, g_final.reshape(1, d))


def _shift_rows(x, d, fill):
    rolled = pltpu.roll(x, d, 0)
    row = lax.broadcasted_iota(jnp.int32, x.shape, 0)
    return jnp.where(row >= d, rolled, fill)


def _recmix_kernel(u_ref, caw_ref, cab_ref, lng_ref, lnb_ref, cbw_ref, cbb_ref,
                   wa_ref, ba_ref, wx_ref, bx_ref, lam_ref, o_ref,
                   glu_buf, conv_buf, xr_buf, hc_ref, *, ts, ch):
    n_chunks = ch // LANES
    ka = caw_ref.shape[0]
    kb = cbw_ref.shape[0]

    @pl.when(pl.program_id(0) == 0)
    def _():
        glu_buf[0:CONV_A_HALO, :] = jnp.zeros((CONV_A_HALO, ch), F32)
        xr_buf[0:CONV_B_HALO, :] = jnp.zeros((CONV_B_HALO, ch), F32)
        hc_ref[...] = jnp.zeros(hc_ref.shape, F32)

    glu_buf[CONV_A_HALO:CONV_A_HALO + ts, :] = u_ref[:, 0:ch] * _sigmoid(u_ref[:, ch:2 * ch])
    offs = [CONV_A_HALO - (ka - 1) + k for k in range(ka)]
    rows = ts + CONV_A_HALO
    for c in range(n_chunks):
        cs = slice(c * LANES, (c + 1) * LANES)
        hist = glu_buf[:, cs]
        acc = jnp.broadcast_to(cab_ref[:, cs], (ts, LANES))
        for r in range(SUBLANES):
            taps = [k for k in range(ka) if offs[k] % SUBLANES == r]
            if not taps:
                continue
            shifted = hist if r == 0 else pltpu.roll(hist, rows - r, 0)
            for k in taps:
                acc = acc + caw_ref[k:k + 1, cs] * shifted[offs[k] - r:offs[k] - r + ts, :]
        conv_buf[:, cs] = acc
    glu_buf[0:CONV_A_HALO, :] = glu_buf[ts:ts + CONV_A_HALO, :]

    ya = conv_buf[...]
    mu = jnp.mean(ya, axis=-1, keepdims=True)
    yc = ya - mu
    var = jnp.mean(yc * yc, axis=-1, keepdims=True)
    yn = yc * lax.rsqrt(var + EPS) * lng_ref[...] + lnb_ref[...]
    o_ref[:, 0:ch] = (yn * _sigmoid(yn)).astype(o_ref.dtype)

    xr_buf[CONV_B_HALO:CONV_B_HALO + ts, :] = u_ref[:, 2 * ch:3 * ch]
    for c in range(n_chunks):
        cs = slice(c * LANES, (c + 1) * LANES)
        xc = jnp.broadcast_to(cbb_ref[:, cs], (ts, LANES))
        for k in range(kb):
            off = CONV_B_HALO - (kb - 1) + k
            xc = xc + cbw_ref[k:k + 1, cs] * xr_buf[off:off + ts, cs]
        xcb = xc.astype(BF16)
        r = _sigmoid(jnp.dot(xcb, wa_ref[c], preferred_element_type=F32) + ba_ref[:, cs])
        gi = _sigmoid(jnp.dot(xcb, wx_ref[c], preferred_element_type=F32) + bx_ref[:, cs])
        log_a = (-RG_C) * r * _softplus(-lam_ref[:, cs])
        a = jnp.exp(log_a)
        uu = jnp.sqrt(1.0 - jnp.exp(2.0 * log_a)) * (gi * xc)
        d = 1
        while d < ts:
            a_prev = _shift_rows(a, d, 1.0)
            u_prev = _shift_rows(uu, d, 0.0)
            uu = a * u_prev + uu
            a = a * a_prev
            d *= 2
        h = uu + a * hc_ref[:, cs]
        hc_ref[:, cs] = h[ts - 1:ts, :]
        gr = u_ref[:, 3 * ch + c * LANES:3 * ch + (c + 1) * LANES]
        gelu = 0.5 * gr * (1.0 + jnp.tanh(math.sqrt(2.0 / math.pi) * (gr + 0.044715 * (gr * gr * gr))))
        o_ref[:, ch + c * LANES:ch + (c + 1) * LANES] = (h * gelu).astype(o_ref.dtype)
    xr_buf[0:CONV_B_HALO, :] = xr_buf[ts:ts + CONV_B_HALO, :]


def _recmix(u, caw, cab, lng, lnb, cbw, cbb, wa, ba, wx, bx, lam, layer, ts=256):
    s, w4 = u.shape
    ch = w4 // 4
    row = lambda v: v.reshape(1, ch)
    const2 = lambda i: (0, 0)
    gate_spec = pl.BlockSpec((None,) + wa.shape[1:], lambda i: (layer, 0, 0, 0))
    return pl.pallas_call(
        functools.partial(_recmix_kernel, ts=ts, ch=ch),
        grid=(s // ts,),
        in_specs=[
            pl.BlockSpec((ts, w4), lambda i: (i, 0)),
            pl.BlockSpec(caw.shape, const2),
            pl.BlockSpec((1, ch), const2),
            pl.BlockSpec((1, ch), const2),
            pl.BlockSpec((1, ch), const2),
            pl.BlockSpec(cbw.shape, const2),
            pl.BlockSpec((1, ch), const2),
            gate_spec,
            pl.BlockSpec((1, ch), const2),
            gate_spec,
            pl.BlockSpec((1, ch), const2),
            pl.BlockSpec((1, ch), const2),
        ],
        out_specs=pl.BlockSpec((ts, 2 * ch), lambda i: (i, 0)),
        out_shape=jax.ShapeDtypeStruct((s, 2 * ch), BF16),
        scratch_shapes=[
            pltpu.VMEM((ts + CONV_A_HALO, ch), F32),
            pltpu.VMEM((ts, ch), F32),
            pltpu.VMEM((ts + CONV_B_HALO, ch), F32),
            pltpu.VMEM((1, ch), F32),
        ],
        compiler_params=_params(("arbitrary",)),
        name="recmix",
    )(u, caw, row(cab), row(lng), row(lnb), cbw, row(cbb), wa, row(ba), wx, row(bx), row(lam))


ATTN_HEADS_PER_STEP = 2
ATTN_KEY_BLOCKS_PER_TRIP = 4
LOG2E = 1.4426950408889634
SIGN_BIT = -2147483648


def _attn_kernel(q_ref, k_ref, v_ref, ntri_ref, o_ref, acc_ref, carry_ref, *, bq, bk, dh):
    qi = pl.program_id(1)
    ntri = ntri_ref[...]
    nt = (((1,), (1,)), ((), ()))
    heads = range(ATTN_HEADS_PER_STEP)
    step = ATTN_KEY_BLOCKS_PER_TRIP

    def block_ref(ref, hh, kb):
        return ref[pl.ds(pl.multiple_of(kb * bk, bk), bk), hh * dh:(hh + 1) * dh]

    def sweep(kbs, masks=None, rows=slice(None)):
        masks = masks or [None] * len(kbs)
        zs = [[lax.dot_general(q_ref[rows, hh * dh:(hh + 1) * dh], block_ref(k_ref, hh, kb), nt,
                               preferred_element_type=F32) for kb in kbs] for hh in heads]
        xs, sums = [], []
        for hh in heads:
            newer = None
            row = []
            for n in range(len(kbs)):
                z = zs[hh][n]
                t = lax.bitcast_convert_type(lax.bitcast_convert_type(z, jnp.int32) | SIGN_BIT, F32)
                sp = jnp.maximum(z, 0.0) + jnp.log(1.0 + jnp.exp2(t)) * LOG2E
                if masks[n] is not None:
                    sp = jnp.where(masks[n], sp, 0.0)
                tail = jnp.dot(sp.astype(BF16), ntri, preferred_element_type=F32)
                x = z + tail
                if newer is not None:
                    x = x + newer
                row.append(x)
                newer = tail[:, 0:1] if newer is None else newer + tail[:, 0:1]
            xs.append(row)
            sums.append(newer)
        v_rows = pl.ds(pl.multiple_of(kbs[-1] * bk, bk), len(kbs) * bk)
        for hh in heads:
            ws = []
            for n in reversed(range(len(kbs))):
                w = jnp.exp2(xs[hh][n])
                if masks[n] is not None:
                    w = jnp.where(masks[n], w, 0.0)
                ws.append(w.astype(BF16))
            pv = jnp.dot(jnp.concatenate(ws, axis=1), v_ref[v_rows, hh * dh:(hh + 1) * dh],
                         preferred_element_type=F32)
            acc_ref[hh, rows] = acc_ref[hh, rows] + jnp.exp2(carry_ref[hh, rows]) * pv
            carry_ref[hh, rows] = carry_ref[hh, rows] + sums[hh]

    acc_ref[...] = jnp.zeros(acc_ref.shape, F32)
    carry_ref[...] = jnp.zeros(carry_ref.shape, F32)

    below_diagonal = (lax.broadcasted_iota(jnp.int32, (bk, bk), 1)
                      < lax.broadcasted_iota(jnp.int32, (bk, bk), 0))
    for r in range(step):
        sweep([step * qi + r - m for m in range(r + 1)], [below_diagonal] + [None] * r,
              rows=slice(r * bk, (r + 1) * bk))

    def older(j, _):
        newest = step * (qi - j) - 1
        sweep([newest - n for n in range(step)])
        return 0

    lax.fori_loop(0, qi, older, 0)

    for hh in heads:
        o_ref[:, hh * dh:(hh + 1) * dh] = acc_ref[hh].astype(o_ref.dtype)


def _attention(qkv, n_heads, bq=1024, bk=256):
    s, w3 = qkv.shape
    dh = w3 // (3 * n_heads)
    hps = ATTN_HEADS_PER_STEP
    groups = n_heads // hps
    assert bq == ATTN_KEY_BLOCKS_PER_TRIP * bk
    tri = (jnp.arange(bk)[:, None] >= jnp.arange(bk)[None, :])
    ntri = -tri.astype(BF16)
    return pl.pallas_call(
        functools.partial(_attn_kernel, bq=bq, bk=bk, dh=dh),
        grid=(groups, s // bq),
        in_specs=[
            pl.BlockSpec((bq, hps * dh), lambda g, i: (i, g)),
            pl.BlockSpec((s, hps * dh), lambda g, i: (0, groups + g)),
            pl.BlockSpec((s, hps * dh), lambda g, i: (0, 2 * groups + g)),
            pl.BlockSpec((bk, bk), lambda g, i: (0, 0)),
        ],
        out_specs=pl.BlockSpec((bq, hps * dh), lambda g, i: (i, g)),
        out_shape=jax.ShapeDtypeStruct((s, n_heads * dh), BF16),
        scratch_shapes=[pltpu.VMEM((hps, bq, dh), F32), pltpu.VMEM((hps, bq, 1), F32)],
        compiler_params=_params(("parallel", "arbitrary")),
        name="sb_attention",
    )(qkv, qkv, qkv, ntri)


def kernel(x, p, norm_mix_g, norm_mlp_g, norm_ple_g, norm_f_g, w_in_rec, conv_a_w, conv_a_b, ln_a_g, ln_a_b, conv_b_w, conv_b_b, w_rg_a, b_rg_a, w_rg_x, b_rg_x, rg_lambda, w_out_rec, w_qkv, w_o_attn, w_mlp_up, w_mlp_down, w_ple_proj, w_ple_gate):
    bsz, seq, d = x.shape
    depth = p.shape[0]
    outs = []
    w_in_rec, w_out_rec, w_qkv, w_o_attn, w_ple_proj, w_ple_gate, w_rg_a, w_rg_x = (
        w.astype(BF16) for w in (w_in_rec, w_out_rec, w_qkv, w_o_attn,
                                 w_ple_proj, w_ple_gate, w_rg_a, w_rg_x))
    q_scale = LOG2E / math.sqrt(d // SB_HEADS)
    qkv_scale = jnp.concatenate([jnp.full((d,), q_scale, F32), jnp.ones((2 * d,), F32)])
    for b in range(bsz):
        h = x[b]
        for i in range(depth):
            j = i // 2
            if i % 2 == 0:
                u = _norm_matmul(h, norm_mix_g[i], w_in_rec, j, F32)
                y = _recmix(u, conv_a_w[j], conv_a_b[j], ln_a_g[j], ln_a_b[j], conv_b_w[j], conv_b_b[j],
                            w_rg_a, b_rg_a[j], w_rg_x, b_rg_x[j], rg_lambda[j], j)
                h = _matmul_residual(y, w_out_rec, j, h)
            else:
                qkv = _norm_matmul(h, norm_mix_g[i], w_qkv, j, BF16, col_scale=qkv_scale)
                o = _attention(qkv, SB_HEADS)
                h = _matmul_residual(o, w_o_attn, j, h)
            h = _mlp(h, norm_mlp_g[i], w_mlp_up, w_mlp_down, i)
            h = _ple(h, norm_ple_g[i], w_ple_gate, p, w_ple_proj, i, b, norm_f_g,
                     final_norm=(i == depth - 1))
        outs.append(h)
    return jnp.stack(outs, axis=0)
```

```python
import functools
import math

import jax
import jax.numpy as jnp
from jax import lax
from jax.experimental import pallas as pl
from jax.experimental.pallas import tpu as pltpu

F32 = jnp.float32
BF16 = jnp.bfloat16

EPS = 1e-6
RG_C = 8.0
SB_HEADS = 16
CONV_A_HALO = 32
CONV_B_HALO = 8
LANES = 128
SUBLANES = 8
VMEM_LIMIT = 56 * 1024 * 1024


def _params(semantics):
    return pltpu.CompilerParams(dimension_semantics=semantics, vmem_limit_bytes=VMEM_LIMIT)


def _rms_normed(x, g):
    ms = jnp.mean(x * x, axis=-1, keepdims=True)
    return x * lax.rsqrt(ms + EPS) * g


def _sigmoid(x):
    return 1.0 / (1.0 + jnp.exp(-x))


def _softplus(x):
    return jnp.maximum(x, 0.0) + jnp.log(1.0 + jnp.exp(-jnp.abs(x)))


def _norm_matmul_kernel(x_ref, g_ref, w_ref, *rest, scaled):
    cs_ref = rest[0] if scaled else None
    o_ref, xn_ref = rest[-2:]

    @pl.when(pl.program_id(1) == 0)
    def _():
        xn_ref[...] = _rms_normed(x_ref[...], g_ref[...]).astype(BF16)

    acc = jnp.dot(xn_ref[...], w_ref[...], preferred_element_type=F32)
    if scaled:
        acc = acc * cs_ref[...]
    o_ref[...] = acc.astype(o_ref.dtype)


def _norm_matmul(x, g, w, layer, out_dtype, col_scale=None, tm=1024, tn=1024):
    m, d = x.shape
    n = w.shape[2]
    scaled = col_scale is not None
    in_specs = [
        pl.BlockSpec((tm, d), lambda i, j: (i, 0)),
        pl.BlockSpec((1, d), lambda i, j: (0, 0)),
        pl.BlockSpec((None, d, tn), lambda i, j: (layer, 0, j)),
    ]
    args = [x, g.reshape(1, d), w]
    if scaled:
        in_specs.append(pl.BlockSpec((1, tn), lambda i, j: (0, j)))
        args.append(col_scale.reshape(1, n))
    return pl.pallas_call(
        functools.partial(_norm_matmul_kernel, scaled=scaled),
        grid=(m // tm, n // tn),
        in_specs=in_specs,
        out_specs=pl.BlockSpec((tm, tn), lambda i, j: (i, j)),
        out_shape=jax.ShapeDtypeStruct((m, n), out_dtype),
        scratch_shapes=[pltpu.VMEM((tm, d), BF16)],
        compiler_params=_params(("parallel", "arbitrary")),
        name="norm_matmul",
    )(*args)


def _matmul_residual_kernel(y_ref, w_ref, h_ref, o_ref):
    o_ref[...] = h_ref[...] + jnp.dot(y_ref[...], w_ref[...], preferred_element_type=F32)


def _matmul_residual(y, w, layer, h, tm=1024):
    m, k = y.shape
    n = w.shape[2]
    return pl.pallas_call(
        _matmul_residual_kernel,
        grid=(m // tm,),
        in_specs=[
            pl.BlockSpec((tm, k), lambda i: (i, 0)),
            pl.BlockSpec((None, k, n), lambda i: (layer, 0, 0), pipeline_mode=pl.Buffered(1)),
            pl.BlockSpec((tm, n), lambda i: (i, 0)),
        ],
        out_specs=pl.BlockSpec((tm, n), lambda i: (i, 0)),
        out_shape=jax.ShapeDtypeStruct((m, n), F32),
        compiler_params=_params(("parallel",)),
        name="matmul_residual",
    )(y, w, h)


def _mlp_kernel(h_ref, g_ref, wu_ref, wd_ref, o_ref, xn_ref):
    @pl.when(pl.program_id(1) == 0)
    def _():
        h = h_ref[...]
        xn_ref[...] = _rms_normed(h, g_ref[...]).astype(BF16)
        o_ref[...] = h

    a = jnp.dot(xn_ref[...], wu_ref[...].astype(BF16), preferred_element_type=F32)
    a = jnp.maximum(a, 0.0)
    a = (a * a).astype(BF16)
    o_ref[...] += jnp.dot(a, wd_ref[...].astype(BF16), preferred_element_type=F32)


def _mlp(h, g, w_up, w_down, layer, tm=1024, tf=512):
    m, d = h.shape
    f = w_up.shape[2]
    return pl.pallas_call(
        _mlp_kernel,
        grid=(m // tm, f // tf),
        in_specs=[
            pl.BlockSpec((tm, d), lambda i, j: (i, 0)),
            pl.BlockSpec((1, d), lambda i, j: (0, 0)),
            pl.BlockSpec((None, d, tf), lambda i, j: (layer, 0, j)),
            pl.BlockSpec((None, tf, d), lambda i, j: (layer, j, 0)),
        ],
        out_specs=pl.BlockSpec((tm, d), lambda i, j: (i, 0)),
        out_shape=jax.ShapeDtypeStruct((m, d), F32),
        scratch_shapes=[pltpu.VMEM((tm, d), BF16)],
        compiler_params=_params(("parallel", "arbitrary")),
        name="mlp",
    )(h, g.reshape(1, d), w_up, w_down)


def _ple_kernel(h_ref, g_ref, wg_ref, p_ref, wp_ref, gf_ref, o_ref, *, final_norm):
    h = h_ref[...]
    xn = _rms_normed(h, g_ref[...]).astype(BF16)
    gate = _sigmoid(jnp.dot(xn, wg_ref[...], preferred_element_type=F32))
    emb = jnp.dot(p_ref[...].astype(BF16), wp_ref[...], preferred_element_type=F32)
    out = h + emb * gate
    if final_norm:
        out = _rms_normed(out, gf_ref[...])
    o_ref[...] = out


def _ple(h, g, w_gate, p, w_proj, layer, batch, g_final, final_norm, tm=512):
    m, d = h.shape
    pd = p.shape[3]
    return pl.pallas_call(
        functools.partial(_ple_kernel, final_norm=final_norm),
        grid=(m // tm,),
        in_specs=[
            pl.BlockSpec((tm, d), lambda i: (i, 0)),
            pl.BlockSpec((1, d), lambda i: (0, 0)),
            pl.BlockSpec((None, d, d), lambda i: (layer, 0, 0)),
            pl.BlockSpec((None, None, tm, pd), lambda i: (layer, batch, i, 0)),
            pl.BlockSpec((None, pd, d), lambda i: (layer, 0, 0)),
            pl.BlockSpec((1, d), lambda i: (0, 0)),
        ],
        out_specs=pl.BlockSpec((tm, d), lambda i: (i, 0)),
        out_shape=jax.ShapeDtypeStruct((m, d), F32),
        compiler_params=_params(("parallel",)),
        name="ple",
    )(h, g.reshape(1, d), w_gate, p, w_proj, g_final.reshape(1, d))


def _shift_rows(x, d, fill):
    rolled = pltpu.roll(x, d, 0)
    row = lax.broadcasted_iota(jnp.int32, x.shape, 0)
    return jnp.where(row >= d, rolled, fill)


def _recmix_kernel(u_ref, caw_ref, cab_ref, lng_ref, lnb_ref, cbw_ref, cbb_ref,
                   wa_ref, ba_ref, wx_ref, bx_ref, lam_ref, o_ref,
                   glu_buf, conv_buf, xr_buf, hc_ref, *, ts, ch):
    n_chunks = ch // LANES
    ka = caw_ref.shape[0]
    kb = cbw_ref.shape[0]

    @pl.when(pl.program_id(0) == 0)
    def _():
        glu_buf[0:CONV_A_HALO, :] = jnp.zeros((CONV_A_HALO, ch), F32)
        xr_buf[0:CONV_B_HALO, :] = jnp.zeros((CONV_B_HALO, ch), F32)
        hc_ref[...] = jnp.zeros(hc_ref.shape, F32)

    glu_buf[CONV_A_HALO:CONV_A_HALO + ts, :] = u_ref[:, 0:ch] * _sigmoid(u_ref[:, ch:2 * ch])
    offs = [CONV_A_HALO - (ka - 1) + k for k in range(ka)]
    rows = ts + CONV_A_HALO
    for c in range(n_chunks):
        cs = slice(c * LANES, (c + 1) * LANES)
        hist = glu_buf[:, cs]
        acc = jnp.broadcast_to(cab_ref[:, cs], (ts, LANES))
        for r in range(SUBLANES):
            taps = [k for k in range(ka) if offs[k] % SUBLANES == r]
            if not taps:
                continue
            shifted = hist if r == 0 else pltpu.roll(hist, rows - r, 0)
            for k in taps:
                acc = acc + caw_ref[k:k + 1, cs] * shifted[offs[k] - r:offs[k] - r + ts, :]
        conv_buf[:, cs] = acc
    glu_buf[0:CONV_A_HALO, :] = glu_buf[ts:ts + CONV_A_HALO, :]

    ya = conv_buf[...]
    mu = jnp.mean(ya, axis=-1, keepdims=True)
    yc = ya - mu
    var = jnp.mean(yc * yc, axis=-1, keepdims=True)
    yn = yc * lax.rsqrt(var + EPS) * lng_ref[...] + lnb_ref[...]
    o_ref[:, 0:ch] = (yn * _sigmoid(yn)).astype(o_ref.dtype)

    xr_buf[CONV_B_HALO:CONV_B_HALO + ts, :] = u_ref[:, 2 * ch:3 * ch]
    for c in range(n_chunks):
        cs = slice(c * LANES, (c + 1) * LANES)
        xc = jnp.broadcast_to(cbb_ref[:, cs], (ts, LANES))
        for k in range(kb):
            off = CONV_B_HALO - (kb - 1) + k
            xc = xc + cbw_ref[k:k + 1, cs] * xr_buf[off:off + ts, cs]
        xcb = xc.astype(BF16)
        r = _sigmoid(jnp.dot(xcb, wa_ref[c], preferred_element_type=F32) + ba_ref[:, cs])
        gi = _sigmoid(jnp.dot(xcb, wx_ref[c], preferred_element_type=F32) + bx_ref[:, cs])
        log_a = (-RG_C) * r * _softplus(-lam_ref[:, cs])
        a = jnp.exp(log_a)
        uu = jnp.sqrt(1.0 - jnp.exp(2.0 * log_a)) * (gi * xc)
        d = 1
        while d < ts:
            a_prev = _shift_rows(a, d, 1.0)
            u_prev = _shift_rows(uu, d, 0.0)
            uu = a * u_prev + uu
            a = a * a_prev
            d *= 2
        h = uu + a * hc_ref[:, cs]
        hc_ref[:, cs] = h[ts - 1:ts, :]
        gr = u_ref[:, 3 * ch + c * LANES:3 * ch + (c + 1) * LANES]
        gelu = 0.5 * gr * (1.0 + jnp.tanh(math.sqrt(2.0 / math.pi) * (gr + 0.044715 * (gr * gr * gr))))
        o_ref[:, ch + c * LANES:ch + (c + 1) * LANES] = (h * gelu).astype(o_ref.dtype)
    xr_buf[0:CONV_B_HALO, :] = xr_buf[ts:ts + CONV_B_HALO, :]


def _recmix(u, caw, cab, lng, lnb, cbw, cbb, wa, ba, wx, bx, lam, layer, ts=256):
    s, w4 = u.shape
    ch = w4 // 4
    row = lambda v: v.reshape(1, ch)
    const2 = lambda i: (0, 0)
    gate_spec = pl.BlockSpec((None,) + wa.shape[1:], lambda i: (layer, 0, 0, 0))
    return pl.pallas_call(
        functools.partial(_recmix_kernel, ts=ts, ch=ch),
        grid=(s // ts,),
        in_specs=[
            pl.BlockSpec((ts, w4), lambda i: (i, 0)),
            pl.BlockSpec(caw.shape, const2),
            pl.BlockSpec((1, ch), const2),
            pl.BlockSpec((1, ch), const2),
            pl.BlockSpec((1, ch), const2),
            pl.BlockSpec(cbw.shape, const2),
            pl.BlockSpec((1, ch), const2),
            gate_spec,
            pl.BlockSpec((1, ch), const2),
            gate_spec,
            pl.BlockSpec((1, ch), const2),
            pl.BlockSpec((1, ch), const2),
        ],
        out_specs=pl.BlockSpec((ts, 2 * ch), lambda i: (i, 0)),
        out_shape=jax.ShapeDtypeStruct((s, 2 * ch), BF16),
        scratch_shapes=[
            pltpu.VMEM((ts + CONV_A_HALO, ch), F32),
            pltpu.VMEM((ts, ch), F32),
            pltpu.VMEM((ts + CONV_B_HALO, ch), F32),
            pltpu.VMEM((1, ch), F32),
        ],
        compiler_params=_params(("arbitrary",)),
        name="recmix",
    )(u, caw, row(cab), row(lng), row(lnb), cbw, row(cbb), wa, row(ba), wx, row(bx), row(lam))


ATTN_HEADS_PER_STEP = 2
ATTN_KEY_BLOCKS_PER_TRIP = 4
LOG2E = 1.4426950408889634
SIGN_BIT = -2147483648


def _attn_kernel(q_ref, k_ref, v_ref, ntri_ref, o_ref, acc_ref, carry_ref, *, bq, bk, dh):
    qi = pl.program_id(1)
    ntri = ntri_ref[...]
    nt = (((1,), (1,)), ((), ()))
    heads = range(ATTN_HEADS_PER_STEP)
    step = ATTN_KEY_BLOCKS_PER_TRIP

    def block_ref(ref, hh, kb):
        return ref[pl.ds(pl.multiple_of(kb * bk, bk), bk), hh * dh:(hh + 1) * dh]

    def sweep(kbs, masks=None, rows=slice(None)):
        masks = masks or [None] * len(kbs)
        zs = [[lax.dot_general(q_ref[rows, hh * dh:(hh + 1) * dh], block_ref(k_ref, hh, kb), nt,
                               preferred_element_type=F32) for kb in kbs] for hh in heads]
        xs, sums = [], []
        for hh in heads:
            newer = None
            row = []
            for n in range(len(kbs)):
                z = zs[hh][n]
                t = lax.bitcast_convert_type(lax.bitcast_convert_type(z, jnp.int32) | SIGN_BIT, F32)
                sp = jnp.maximum(z, 0.0) + jnp.log(1.0 + jnp.exp2(t)) * LOG2E
                if masks[n] is not None:
                    sp = jnp.where(masks[n], sp, 0.0)
                tail = jnp.dot(sp.astype(BF16), ntri, preferred_element_type=F32)
                x = z + tail
                if newer is not None:
                    x = x + newer
                row.append(x)
                newer = tail[:, 0:1] if newer is None else newer + tail[:, 0:1]
            xs.append(row)
            sums.append(newer)
        v_rows = pl.ds(pl.multiple_of(kbs[-1] * bk, bk), len(kbs) * bk)
        for hh in heads:
            ws = []
            for n in reversed(range(len(kbs))):
                w = jnp.exp2(xs[hh][n])
                if masks[n] is not None:
                    w = jnp.where(masks[n], w, 0.0)
                ws.append(w.astype(BF16))
            pv = jnp.dot(jnp.concatenate(ws, axis=1), v_ref[v_rows, hh * dh:(hh + 1) * dh],
                         preferred_element_type=F32)
            acc_ref[hh, rows] = acc_ref[hh, rows] + jnp.exp2(carry_ref[hh, rows]) * pv
            carry_ref[hh, rows] = carry_ref[hh, rows] + sums[hh]

    acc_ref[...] = jnp.zeros(acc_ref.shape, F32)
    carry_ref[...] = jnp.zeros(carry_ref.shape, F32)

    below_diagonal = (lax.broadcasted_iota(jnp.int32, (bk, bk), 1)
                      < lax.broadcasted_iota(jnp.int32, (bk, bk), 0))
    for r in range(step):
        sweep([step * qi + r - m for m in range(r + 1)], [below_diagonal] + [None] * r,
              rows=slice(r * bk, (r + 1) * bk))

    def older(j, _):
        newest = step * (qi - j) - 1
        sweep([newest - n for n in range(step)])
        return 0

    lax.fori_loop(0, qi, older, 0)

    for hh in heads:
        o_ref[:, hh * dh:(hh + 1) * dh] = acc_ref[hh].astype(o_ref.dtype)


def _attention(qkv, n_heads, bq=1024, bk=256):
    s, w3 = qkv.shape
    dh = w3 // (3 * n_heads)
    hps = ATTN_HEADS_PER_STEP
    groups = n_heads // hps
    assert bq == ATTN_KEY_BLOCKS_PER_TRIP * bk
    tri = (jnp.arange(bk)[:, None] >= jnp.arange(bk)[None, :])
    ntri = -tri.astype(BF16)
    return pl.pallas_call(
        functools.partial(_attn_kernel, bq=bq, bk=bk, dh=dh),
        grid=(groups, s // bq),
        in_specs=[
            pl.BlockSpec((bq, hps * dh), lambda g, i: (i, g)),
            pl.BlockSpec((s, hps * dh), lambda g, i: (0, groups + g)),
            pl.BlockSpec((s, hps * dh), lambda g, i: (0, 2 * groups + g)),
            pl.BlockSpec((bk, bk), lambda g, i: (0, 0)),
        ],
        out_specs=pl.BlockSpec((bq, hps * dh), lambda g, i: (i, g)),
        out_shape=jax.ShapeDtypeStruct((s, n_heads * dh), BF16),
        scratch_shapes=[pltpu.VMEM((hps, bq, dh), F32), pltpu.VMEM((hps, bq, 1), F32)],
        compiler_params=_params(("parallel", "arbitrary")),
        name="sb_attention",
    )(qkv, qkv, qkv, ntri)


def kernel(x, p, norm_mix_g, norm_mlp_g, norm_ple_g, norm_f_g, w_in_rec, conv_a_w, conv_a_b, ln_a_g, ln_a_b, conv_b_w, conv_b_b, w_rg_a, b_rg_a, w_rg_x, b_rg_x, rg_lambda, w_out_rec, w_qkv, w_o_attn, w_mlp_up, w_mlp_down, w_ple_proj, w_ple_gate):
    bsz, seq, d = x.shape
    depth = p.shape[0]
    outs = []
    w_in_rec, w_out_rec, w_qkv, w_o_attn, w_ple_proj, w_ple_gate, w_rg_a, w_rg_x = (
        w.astype(BF16) for w in (w_in_rec, w_out_rec, w_qkv, w_o_attn,
                                 w_ple_proj, w_ple_gate, w_rg_a, w_rg_x))
    q_scale = LOG2E / math.sqrt(d // SB_HEADS)
    qkv_scale = jnp.concatenate([jnp.full((d,), q_scale, F32), jnp.ones((2 * d,), F32)])
    for b in range(bsz):
        h = x[b]
        for i in range(depth):
            j = i // 2
            if i % 2 == 0:
                u = _norm_matmul(h, norm_mix_g[i], w_in_rec, j, F32)
                y = _recmix(u, conv_a_w[j], conv_a_b[j], ln_a_g[j], ln_a_b[j], conv_b_w[j], conv_b_b[j],
                            w_rg_a, b_rg_a[j], w_rg_x, b_rg_x[j], rg_lambda[j], j)
                h = _matmul_residual(y, w_out_rec, j, h)
            else:
                qkv = _norm_matmul(h, norm_mix_g[i], w_qkv, j, BF16, col_scale=qkv_scale)
                o = _attention(qkv, SB_HEADS)
                h = _matmul_residual(o, w_o_attn, j, h)
            h = _mlp(h, norm_mlp_g[i], w_mlp_up, w_mlp_down, i)
            h = _ple(h, norm_ple_g[i], w_ple_gate, p, w_ple_proj, i, b, norm_f_g,
                     final_norm=(i == depth - 1))
        outs.append(h)
    return jnp.stack(outs, axis=0)
```

```python
import functools
import math

import jax
import jax.numpy as jnp
from jax import lax
from jax.experimental import pallas as pl
from jax.experimental.pallas import tpu as pltpu

F32 = jnp.float32
BF16 = jnp.bfloat16

EPS = 1e-6
RG_C = 8.0
SB_HEADS = 16
CONV_A_HALO = 32
CONV_B_HALO = 8
LANES = 128
SUBLANES = 8
VMEM_LIMIT = 56 * 1024 * 1024


def _params(semantics):
    return pltpu.CompilerParams(dimension_semantics=semantics, vmem_limit_bytes=VMEM_LIMIT)


def _rms_normed(x, g):
    ms = jnp.mean(x * x, axis=-1, keepdims=True)
    return x * lax.rsqrt(ms + EPS) * g


def _sigmoid(x):
    return 1.0 / (1.0 + jnp.exp(-x))


def _softplus(x):
    return jnp.maximum(x, 0.0) + jnp.log(1.0 + jnp.exp(-jnp.abs(x)))


def _norm_matmul_kernel(x_ref, g_ref, w_ref, *rest, scaled):
    cs_ref = rest[0] if scaled else None
    o_ref, xn_ref = rest[-2:]

    @pl.when(pl.program_id(1) == 0)
    def _():
        xn_ref[...] = _rms_normed(x_ref[...], g_ref[...]).astype(BF16)

    acc = jnp.dot(xn_ref[...], w_ref[...], preferred_element_type=F32)
    if scaled:
        acc = acc * cs_ref[...]
    o_ref[...] = acc.astype(o_ref.dtype)


def _norm_matmul(x, g, w, layer, out_dtype, col_scale=None, tm=1024, tn=1024):
    m, d = x.shape
    n = w.shape[2]
    scaled = col_scale is not None
    in_specs = [
        pl.BlockSpec((tm, d), lambda i, j: (i, 0)),
        pl.BlockSpec((1, d), lambda i, j: (0, 0)),
        pl.BlockSpec((None, d, tn), lambda i, j: (layer, 0, j)),
    ]
    args = [x, g.reshape(1, d), w]
    if scaled:
        in_specs.append(pl.BlockSpec((1, tn), lambda i, j: (0, j)))
        args.append(col_scale.reshape(1, n))
    return pl.pallas_call(
        functools.partial(_norm_matmul_kernel, scaled=scaled),
        grid=(m // tm, n // tn),
        in_specs=in_specs,
        out_specs=pl.BlockSpec((tm, tn), lambda i, j: (i, j)),
        out_shape=jax.ShapeDtypeStruct((m, n), out_dtype),
        scratch_shapes=[pltpu.VMEM((tm, d), BF16)],
        compiler_params=_params(("parallel", "arbitrary")),
        name="norm_matmul",
    )(*args)


def _matmul_residual_kernel(y_ref, w_ref, h_ref, o_ref):
    o_ref[...] = h_ref[...] + jnp.dot(y_ref[...], w_ref[...], preferred_element_type=F32)


def _matmul_residual(y, w, layer, h, tm=512):
    m, k = y.shape
    n = w.shape[2]
    return pl.pallas_call(
        _matmul_residual_kernel,
        grid=(m // tm,),
        in_specs=[
            pl.BlockSpec((tm, k), lambda i: (i, 0)),
            pl.BlockSpec((None, k, n), lambda i: (layer, 0, 0)),
            pl.BlockSpec((tm, n), lambda i: (i, 0)),
        ],
        out_specs=pl.BlockSpec((tm, n), lambda i: (i, 0)),
        out_shape=jax.ShapeDtypeStruct((m, n), F32),
        compiler_params=_params(("parallel",)),
        name="matmul_residual",
    )(y, w, h)


def _mlp_kernel(h_ref, g_ref, wu_ref, wd_ref, o_ref, xn_ref):
    @pl.when(pl.program_id(1) == 0)
    def _():
        h = h_ref[...]
        xn_ref[...] = _rms_normed(h, g_ref[...]).astype(BF16)
        o_ref[...] = h

    a = jnp.dot(xn_ref[...], wu_ref[...].astype(BF16), preferred_element_type=F32)
    a = jnp.maximum(a, 0.0)
    a = (a * a).astype(BF16)
    o_ref[...] += jnp.dot(a, wd_ref[...].astype(BF16), preferred_element_type=F32)


def _mlp(h, g, w_up, w_down, layer, tm=1024, tf=512):
    m, d = h.shape
    f = w_up.shape[2]
    return pl.pallas_call(
        _mlp_kernel,
        grid=(m // tm, f // tf),
        in_specs=[
            pl.BlockSpec((tm, d), lambda i, j: (i, 0)),
            pl.BlockSpec((1, d), lambda i, j: (0, 0)),
            pl.BlockSpec((None, d, tf), lambda i, j: (layer, 0, j)),
            pl.BlockSpec((None, tf, d), lambda i, j: (layer, j, 0)),
        ],
        out_specs=pl.BlockSpec((tm, d), lambda i, j: (i, 0)),
        out_shape=jax.ShapeDtypeStruct((m, d), F32),
        scratch_shapes=[pltpu.VMEM((tm, d), BF16)],
        compiler_params=_params(("parallel", "arbitrary")),
        name="mlp",
    )(h, g.reshape(1, d), w_up, w_down)


def _ple_kernel(h_ref, g_ref, wg_ref, p_ref, wp_ref, gf_ref, o_ref, *, final_norm):
    h = h_ref[...]
    xn = _rms_normed(h, g_ref[...]).astype(BF16)
    gate = _sigmoid(jnp.dot(xn, wg_ref[...], preferred_element_type=F32))
    emb = jnp.dot(p_ref[...].astype(BF16), wp_ref[...], preferred_element_type=F32)
    out = h + emb * gate
    if final_norm:
        out = _rms_normed(out, gf_ref[...])
    o_ref[...] = out


def _ple(h, g, w_gate, p, w_proj, layer, batch, g_final, final_norm, tm=256):
    m, d = h.shape
    pd = p.shape[3]
    return pl.pallas_call(
        functools.partial(_ple_kernel, final_norm=final_norm),
        grid=(m // tm,),
        in_specs=[
            pl.BlockSpec((tm, d), lambda i: (i, 0)),
            pl.BlockSpec((1, d), lambda i: (0, 0)),
            pl.BlockSpec((None, d, d), lambda i: (layer, 0, 0)),
            pl.BlockSpec((None, None, tm, pd), lambda i: (layer, batch, i, 0)),
            pl.BlockSpec((None, pd, d), lambda i: (layer, 0, 0)),
            pl.BlockSpec((1, d), lambda i: (0, 0)),
        ],
        out_specs=pl.BlockSpec((tm, d), lambda i: (i, 0)),
        out_shape=jax.ShapeDtypeStruct((m, d), F32),
        compiler_params=_params(("parallel",)),
        name="ple",
    )(h, g.reshape(1, d), w_gate, p, w_proj, g_final.reshape(1, d))


def _shift_rows(x, d, fill):
    rolled = pltpu.roll(x, d, 0)
    row = lax.broadcasted_iota(jnp.int32, x.shape, 0)
    return jnp.where(row >= d, rolled, fill)


def _recmix_kernel(u_ref, caw_ref, cab_ref, lng_ref, lnb_ref, cbw_ref, cbb_ref,
                   wa_ref, ba_ref, wx_ref, bx_ref, lam_ref, o_ref,
                   glu_buf, conv_buf, xr_buf, hc_ref, *, ts, ch):
    n_chunks = ch // LANES
    ka = caw_ref.shape[0]
    kb = cbw_ref.shape[0]

    @pl.when(pl.program_id(0) == 0)
    def _():
        glu_buf[0:CONV_A_HALO, :] = jnp.zeros((CONV_A_HALO, ch), F32)
        xr_buf[0:CONV_B_HALO, :] = jnp.zeros((CONV_B_HALO, ch), F32)
        hc_ref[...] = jnp.zeros(hc_ref.shape, F32)

    glu_buf[CONV_A_HALO:CONV_A_HALO + ts, :] = u_ref[:, 0:ch] * _sigmoid(u_ref[:, ch:2 * ch])
    offs = [CONV_A_HALO - (ka - 1) + k for k in range(ka)]
    rows = ts + CONV_A_HALO
    for c in range(n_chunks):
        cs = slice(c * LANES, (c + 1) * LANES)
        hist = glu_buf[:, cs]
        acc = jnp.broadcast_to(cab_ref[:, cs], (ts, LANES))
        for r in range(SUBLANES):
            taps = [k for k in range(ka) if offs[k] % SUBLANES == r]
            if not taps:
                continue
            shifted = hist if r == 0 else pltpu.roll(hist, rows - r, 0)
            for k in taps:
                acc = acc + caw_ref[k:k + 1, cs] * shifted[offs[k] - r:offs[k] - r + ts, :]
        conv_buf[:, cs] = acc
    glu_buf[0:CONV_A_HALO, :] = glu_buf[ts:ts + CONV_A_HALO, :]

    ya = conv_buf[...]
    mu = jnp.mean(ya, axis=-1, keepdims=True)
    yc = ya - mu
    var = jnp.mean(yc * yc, axis=-1, keepdims=True)
    yn = yc * lax.rsqrt(var + EPS) * lng_ref[...] + lnb_ref[...]
    o_ref[:, 0:ch] = (yn * _sigmoid(yn)).astype(o_ref.dtype)

    xr_buf[CONV_B_HALO:CONV_B_HALO + ts, :] = u_ref[:, 2 * ch:3 * ch]
    for c in range(n_chunks):
        cs = slice(c * LANES, (c + 1) * LANES)
        xc = jnp.broadcast_to(cbb_ref[:, cs], (ts, LANES))
        for k in range(kb):
            off = CONV_B_HALO - (kb - 1) + k
            xc = xc + cbw_ref[k:k + 1, cs] * xr_buf[off:off + ts, cs]
        xcb = xc.astype(BF16)
        r = _sigmoid(jnp.dot(xcb, wa_ref[c], preferred_element_type=F32) + ba_ref[:, cs])
        gi = _sigmoid(jnp.dot(xcb, wx_ref[c], preferred_element_type=F32) + bx_ref[:, cs])
        log_a = (-RG_C) * r * _softplus(-lam_ref[:, cs])
        a = jnp.exp(log_a)
        uu = jnp.sqrt(1.0 - jnp.exp(2.0 * log_a)) * (gi * xc)
        d = 1
        while d < ts:
            a_prev = _shift_rows(a, d, 1.0)
            u_prev = _shift_rows(uu, d, 0.0)
            uu = a * u_prev + uu
            a = a * a_prev
            d *= 2
        h = uu + a * hc_ref[:, cs]
        hc_ref[:, cs] = h[ts - 1:ts, :]
        gr = u_ref[:, 3 * ch + c * LANES:3 * ch + (c + 1) * LANES]
        gelu = 0.5 * gr * (1.0 + jnp.tanh(math.sqrt(2.0 / math.pi) * (gr + 0.044715 * (gr * gr * gr))))
        o_ref[:, ch + c * LANES:ch + (c + 1) * LANES] = (h * gelu).astype(o_ref.dtype)
    xr_buf[0:CONV_B_HALO, :] = xr_buf[ts:ts + CONV_B_HALO, :]


def _recmix(u, caw, cab, lng, lnb, cbw, cbb, wa, ba, wx, bx, lam, layer, ts=256):
    s, w4 = u.shape
    ch = w4 // 4
    row = lambda v: v.reshape(1, ch)
    const2 = lambda i: (0, 0)
    gate_spec = pl.BlockSpec((None,) + wa.shape[1:], lambda i: (layer, 0, 0, 0))
    return pl.pallas_call(
        functools.partial(_recmix_kernel, ts=ts, ch=ch),
        grid=(s // ts,),
        in_specs=[
            pl.BlockSpec((ts, w4), lambda i: (i, 0)),
            pl.BlockSpec(caw.shape, const2),
            pl.BlockSpec((1, ch), const2),
            pl.BlockSpec((1, ch), const2),
            pl.BlockSpec((1, ch), const2),
            pl.BlockSpec(cbw.shape, const2),
            pl.BlockSpec((1, ch), const2),
            gate_spec,
            pl.BlockSpec((1, ch), const2),
            gate_spec,
            pl.BlockSpec((1, ch), const2),
            pl.BlockSpec((1, ch), const2),
        ],
        out_specs=pl.BlockSpec((ts, 2 * ch), lambda i: (i, 0)),
        out_shape=jax.ShapeDtypeStruct((s, 2 * ch), BF16),
        scratch_shapes=[
            pltpu.VMEM((ts + CONV_A_HALO, ch), F32),
            pltpu.VMEM((ts, ch), F32),
            pltpu.VMEM((ts + CONV_B_HALO, ch), F32),
            pltpu.VMEM((1, ch), F32),
        ],
        compiler_params=_params(("arbitrary",)),
        name="recmix",
    )(u, caw, row(cab), row(lng), row(lnb), cbw, row(cbb), wa, row(ba), wx, row(bx), row(lam))


ATTN_HEADS_PER_STEP = 2
ATTN_KEY_BLOCKS_PER_TRIP = 4
LOG2E = 1.4426950408889634
SIGN_BIT = -2147483648


def _attn_kernel(q_ref, k_ref, v_ref, ntri_ref, o_ref, acc_ref, carry_ref, *, bq, bk, dh):
    qi = pl.program_id(1)
    ntri = ntri_ref[...]
    nt = (((1,), (1,)), ((), ()))
    heads = range(ATTN_HEADS_PER_STEP)
    step = ATTN_KEY_BLOCKS_PER_TRIP

    def block_ref(ref, hh, kb):
        return ref[pl.ds(pl.multiple_of(kb * bk, bk), bk), hh * dh:(hh + 1) * dh]

    def sweep(kbs, masks=None, rows=slice(None)):
        masks = masks or [None] * len(kbs)
        zs = [[lax.dot_general(q_ref[rows, hh * dh:(hh + 1) * dh], block_ref(k_ref, hh, kb), nt,
                               preferred_element_type=F32) for kb in kbs] for hh in heads]
        xs, sums = [], []
        for hh in heads:
            newer = None
            row = []
            for n in range(len(kbs)):
                z = zs[hh][n]
                t = lax.bitcast_convert_type(lax.bitcast_convert_type(z, jnp.int32) | SIGN_BIT, F32)
                sp = jnp.maximum(z, 0.0) + jnp.log(1.0 + jnp.exp2(t)) * LOG2E
                if masks[n] is not None:
                    sp = jnp.where(masks[n], sp, 0.0)
                tail = jnp.dot(sp.astype(BF16), ntri, preferred_element_type=F32)
                x = z + tail
                if newer is not None:
                    x = x + newer
                row.append(x)
                newer = tail[:, 0:1] if newer is None else newer + tail[:, 0:1]
            xs.append(row)
            sums.append(newer)
        v_rows = pl.ds(pl.multiple_of(kbs[-1] * bk, bk), len(kbs) * bk)
        for hh in heads:
            ws = []
            for n in reversed(range(len(kbs))):
                w = jnp.exp2(xs[hh][n])
                if masks[n] is not None:
                    w = jnp.where(masks[n], w, 0.0)
                ws.append(w.astype(BF16))
            pv = jnp.dot(jnp.concatenate(ws, axis=1), v_ref[v_rows, hh * dh:(hh + 1) * dh],
                         preferred_element_type=F32)
            acc_ref[hh, rows] = acc_ref[hh, rows] + jnp.exp2(carry_ref[hh, rows]) * pv
            carry_ref[hh, rows] = carry_ref[hh, rows] + sums[hh]

    acc_ref[...] = jnp.zeros(acc_ref.shape, F32)
    carry_ref[...] = jnp.zeros(carry_ref.shape, F32)

    below_diagonal = (lax.broadcasted_iota(jnp.int32, (bk, bk), 1)
                      < lax.broadcasted_iota(jnp.int32, (bk, bk), 0))
    for r in range(step):
        sweep([step * qi + r - m for m in range(r + 1)], [below_diagonal] + [None] * r,
              rows=slice(r * bk, (r + 1) * bk))

    def older(j, _):
        newest = step * (qi - j) - 1
        sweep([newest - n for n in range(step)])
        return 0

    lax.fori_loop(0, qi, older, 0)

    for hh in heads:
        o_ref[:, hh * dh:(hh + 1) * dh] = acc_ref[hh].astype(o_ref.dtype)


def _attention(qkv, n_heads, bq=1024, bk=256):
    s, w3 = qkv.shape
    dh = w3 // (3 * n_heads)
    hps = ATTN_HEADS_PER_STEP
    groups = n_heads // hps
    assert bq == ATTN_KEY_BLOCKS_PER_TRIP * bk
    tri = (jnp.arange(bk)[:, None] >= jnp.arange(bk)[None, :])
    ntri = -tri.astype(BF16)
    return pl.pallas_call(
        functools.partial(_attn_kernel, bq=bq, bk=bk, dh=dh),
        grid=(groups, s // bq),
        in_specs=[
            pl.BlockSpec((bq, hps * dh), lambda g, i: (i, g)),
            pl.BlockSpec((s, hps * dh), lambda g, i: (0, groups + g)),
            pl.BlockSpec((s, hps * dh), lambda g, i: (0, 2 * groups + g)),
            pl.BlockSpec((bk, bk), lambda g, i: (0, 0)),
        ],
        out_specs=pl.BlockSpec((bq, hps * dh), lambda g, i: (i, g)),
        out_shape=jax.ShapeDtypeStruct((s, n_heads * dh), BF16),
        scratch_shapes=[pltpu.VMEM((hps, bq, dh), F32), pltpu.VMEM((hps, bq, 1), F32)],
        compiler_params=_params(("parallel", "arbitrary")),
        name="sb_attention",
    )(qkv, qkv, qkv, ntri)


def kernel(x, p, norm_mix_g, norm_mlp_g, norm_ple_g, norm_f_g, w_in_rec, conv_a_w, conv_a_b, ln_a_g, ln_a_b, conv_b_w, conv_b_b, w_rg_a, b_rg_a, w_rg_x, b_rg_x, rg_lambda, w_out_rec, w_qkv, w_o_attn, w_mlp_up, w_mlp_down, w_ple_proj, w_ple_gate):
    bsz, seq, d = x.shape
    depth = p.shape[0]
    outs = []
    w_in_rec, w_out_rec, w_qkv, w_o_attn, w_ple_proj, w_ple_gate, w_rg_a, w_rg_x = (
        w.astype(BF16) for w in (w_in_rec, w_out_rec, w_qkv, w_o_attn,
                                 w_ple_proj, w_ple_gate, w_rg_a, w_rg_x))
    q_scale = LOG2E / math.sqrt(d // SB_HEADS)
    qkv_scale = jnp.concatenate([jnp.full((d,), q_scale, F32), jnp.ones((2 * d,), F32)])
    for b in range(bsz):
        h = x[b]
        for i in range(depth):
            j = i // 2
            if i % 2 == 0:
                u = _norm_matmul(h, norm_mix_g[i], w_in_rec, j, F32)
                y = _recmix(u, conv_a_w[j], conv_a_b[j], ln_a_g[j], ln_a_b[j], conv_b_w[j], conv_b_b[j],
                            w_rg_a, b_rg_a[j], w_rg_x, b_rg_x[j], rg_lambda[j], j)
                h = _matmul_residual(y, w_out_rec, j, h)
            else:
                qkv = _norm_matmul(h, norm_mix_g[i], w_qkv, j, BF16, col_scale=qkv_scale)
                o = _attention(qkv, SB_HEADS)
                h = _matmul_residual(o, w_o_attn, j, h)
            h = _mlp(h, norm_mlp_g[i], w_mlp_up, w_mlp_down, i)
            h = _ple(h, norm_ple_g[i], w_ple_gate, p, w_ple_proj, i, b, norm_f_g,
                     final_norm=(i == depth - 1))
        outs.append(h)
    return jnp.stack(outs, axis=0)
```

```python
import functools
import math

import jax
import jax.numpy as jnp
from jax import lax
from jax.experimental import pallas as pl
from jax.experimental.pallas import tpu as pltpu

F32 = jnp.float32
BF16 = jnp.bfloat16

EPS = 1e-6
RG_C = 8.0
SB_HEADS = 16
CONV_A_HALO = 32
CONV_B_HALO = 8
LANES = 128
SUBLANES = 8
VMEM_LIMIT = 56 * 1024 * 1024


def _params(semantics):
    return pltpu.CompilerParams(dimension_semantics=semantics, vmem_limit_bytes=VMEM_LIMIT)


def _rms_normed(x, g):
    ms = jnp.mean(x * x, axis=-1, keepdims=True)
    return x * lax.rsqrt(ms + EPS) * g


def _sigmoid(x):
    return 1.0 / (1.0 + jnp.exp(-x))


def _softplus(x):
    return jnp.maximum(x, 0.0) + jnp.log(1.0 + jnp.exp(-jnp.abs(x)))


def _norm_matmul_kernel(x_ref, g_ref, w_ref, *rest, scaled):
    cs_ref = rest[0] if scaled else None
    o_ref, xn_ref = rest[-2:]

    @pl.when(pl.program_id(1) == 0)
    def _():
        xn_ref[...] = _rms_normed(x_ref[...], g_ref[...]).astype(BF16)

    acc = jnp.dot(xn_ref[...], w_ref[...], preferred_element_type=F32)
    if scaled:
        acc = acc * cs_ref[...]
    o_ref[...] = acc.astype(o_ref.dtype)


def _norm_matmul(x, g, w, layer, out_dtype, col_scale=None, tm=1024, tn=1024):
    m, d = x.shape
    n = w.shape[2]
    scaled = col_scale is not None
    in_specs = [
        pl.BlockSpec((tm, d), lambda i, j: (i, 0)),
        pl.BlockSpec((1, d), lambda i, j: (0, 0)),
        pl.BlockSpec((None, d, tn), lambda i, j: (layer, 0, j)),
    ]
    args = [x, g.reshape(1, d), w]
    if scaled:
        in_specs.append(pl.BlockSpec((1, tn), lambda i, j: (0, j)))
        args.append(col_scale.reshape(1, n))
    return pl.pallas_call(
        functools.partial(_norm_matmul_kernel, scaled=scaled),
        grid=(m // tm, n // tn),
        in_specs=in_specs,
        out_specs=pl.BlockSpec((tm, tn), lambda i, j: (i, j)),
        out_shape=jax.ShapeDtypeStruct((m, n), out_dtype),
        scratch_shapes=[pltpu.VMEM((tm, d), BF16)],
        compiler_params=_params(("parallel", "arbitrary")),
        name="norm_matmul",
    )(*args)


def _matmul_residual_kernel(y_ref, w_ref, h_ref, o_ref):
    o_ref[...] = h_ref[...] + jnp.dot(y_ref[...], w_ref[...], preferred_element_type=F32)


def _matmul_residual(y, w, layer, h, tm=512):
    m, k = y.shape
    n = w.shape[2]
    return pl.pallas_call(
        _matmul_residual_kernel,
        grid=(m // tm,),
        in_specs=[
            pl.BlockSpec((tm, k), lambda i: (i, 0)),
            pl.BlockSpec((None, k, n), lambda i: (layer, 0, 0)),
            pl.BlockSpec((tm, n), lambda i: (i, 0)),
        ],
        out_specs=pl.BlockSpec((tm, n), lambda i: (i, 0)),
        out_shape=jax.ShapeDtypeStruct((m, n), F32),
        compiler_params=_params(("parallel",)),
        name="matmul_residual",
    )(y, w, h)


def _mlp_kernel(h_ref, g_ref, wu_ref, wd_ref, o_ref, xn_ref):
    @pl.when(pl.program_id(1) == 0)
    def _():
        h = h_ref[...]
        xn_ref[...] = _rms_normed(h, g_ref[...]).astype(BF16)
        o_ref[...] = h

    a = jnp.dot(xn_ref[...], wu_ref[...].astype(BF16), preferred_element_type=F32)
    a = jnp.maximum(a, 0.0)
    a = (a * a).astype(BF16)
    o_ref[...] += jnp.dot(a, wd_ref[...].astype(BF16), preferred_element_type=F32)


def _mlp(h, g, w_up, w_down, layer, tm=1024, tf=512):
    m, d = h.shape
    f = w_up.shape[2]
    return pl.pallas_call(
        _mlp_kernel,
        grid=(m // tm, f // tf),
        in_specs=[
            pl.BlockSpec((tm, d), lambda i, j: (i, 0)),
            pl.BlockSpec((1, d), lambda i, j: (0, 0)),
            pl.BlockSpec((None, d, tf), lambda i, j: (layer, 0, j)),
            pl.BlockSpec((None, tf, d), lambda i, j: (layer, j, 0)),
        ],
        out_specs=pl.BlockSpec((tm, d), lambda i, j: (i, 0)),
        out_shape=jax.ShapeDtypeStruct((m, d), F32),
        scratch_shapes=[pltpu.VMEM((tm, d), BF16)],
        compiler_params=_params(("parallel", "arbitrary")),
        name="mlp",
    )(h, g.reshape(1, d), w_up, w_down)


def _ple_kernel(h_ref, g_ref, wg_ref, p_ref, wp_ref, gf_ref, o_ref, *, final_norm):
    h = h_ref[...]
    xn = _rms_normed(h, g_ref[...]).astype(BF16)
    pb = p_ref[...].astype(BF16)
    half = h.shape[1] // 2
    outs = []
    for c in range(2):
        cols = slice(c * half, (c + 1) * half)
        gate = _sigmoid(jnp.dot(xn, wg_ref[:, cols], preferred_element_type=F32))
        emb = jnp.dot(pb, wp_ref[:, cols], preferred_element_type=F32)
        outs.append(h[:, cols] + emb * gate)
    if final_norm:
        o_ref[...] = _rms_normed(jnp.concatenate(outs, axis=1), gf_ref[...])
    else:
        for c in range(2):
            o_ref[:, c * half:(c + 1) * half] = outs[c]


def _ple(h, g, w_gate, p, w_proj, layer, batch, g_final, final_norm, tm=256):
    m, d = h.shape
    pd = p.shape[3]
    return pl.pallas_call(
        functools.partial(_ple_kernel, final_norm=final_norm),
        grid=(m // tm,),
        in_specs=[
            pl.BlockSpec((tm, d), lambda i: (i, 0)),
            pl.BlockSpec((1, d), lambda i: (0, 0)),
            pl.BlockSpec((None, d, d), lambda i: (layer, 0, 0)),
            pl.BlockSpec((None, None, tm, pd), lambda i: (layer, batch, i, 0)),
            pl.BlockSpec((None, pd, d), lambda i: (layer, 0, 0)),
            pl.BlockSpec((1, d), lambda i: (0, 0)),
        ],
        out_specs=pl.BlockSpec((tm, d), lambda i: (i, 0)),
        out_shape=jax.ShapeDtypeStruct((m, d), F32),
        compiler_params=_params(("parallel",)),
        name="ple",
    )(h, g.reshape(1, d), w_gate, p, w_proj, g_final.reshape(1, d))


def _shift_rows(x, d, fill):
    rolled = pltpu.roll(x, d, 0)
    row = lax.broadcasted_iota(jnp.int32, x.shape, 0)
    return jnp.where(row >= d, rolled, fill)


def _recmix_kernel(u_ref, caw_ref, cab_ref, lng_ref, lnb_ref, cbw_ref, cbb_ref,
                   wa_ref, ba_ref, wx_ref, bx_ref, lam_ref, o_ref,
                   glu_buf, conv_buf, xr_buf, hc_ref, *, ts, ch):
    n_chunks = ch // LANES
    ka = caw_ref.shape[0]
    kb = cbw_ref.shape[0]

    @pl.when(pl.program_id(0) == 0)
    def _():
        glu_buf[0:CONV_A_HALO, :] = jnp.zeros((CONV_A_HALO, ch), F32)
        xr_buf[0:CONV_B_HALO, :] = jnp.zeros((CONV_B_HALO, ch), F32)
        hc_ref[...] = jnp.zeros(hc_ref.shape, F32)

    glu_buf[CONV_A_HALO:CONV_A_HALO + ts, :] = u_ref[:, 0:ch] * _sigmoid(u_ref[:, ch:2 * ch])
    offs = [CONV_A_HALO - (ka - 1) + k for k in range(ka)]
    rows = ts + CONV_A_HALO
    for c in range(n_chunks):
        cs = slice(c * LANES, (c + 1) * LANES)
        hist = glu_buf[:, cs]
        acc = jnp.broadcast_to(cab_ref[:, cs], (ts, LANES))
        for r in range(SUBLANES):
            taps = [k for k in range(ka) if offs[k] % SUBLANES == r]
            if not taps:
                continue
            shifted = hist if r == 0 else pltpu.roll(hist, rows - r, 0)
            for k in taps:
                acc = acc + caw_ref[k:k + 1, cs] * shifted[offs[k] - r:offs[k] - r + ts, :]
        conv_buf[:, cs] = acc
    glu_buf[0:CONV_A_HALO, :] = glu_buf[ts:ts + CONV_A_HALO, :]

    ya = conv_buf[...]
    mu = jnp.mean(ya, axis=-1, keepdims=True)
    yc = ya - mu
    var = jnp.mean(yc * yc, axis=-1, keepdims=True)
    yn = yc * lax.rsqrt(var + EPS) * lng_ref[...] + lnb_ref[...]
    o_ref[:, 0:ch] = (yn * _sigmoid(yn)).astype(o_ref.dtype)

    xr_buf[CONV_B_HALO:CONV_B_HALO + ts, :] = u_ref[:, 2 * ch:3 * ch]
    for c in range(n_chunks):
        cs = slice(c * LANES, (c + 1) * LANES)
        xc = jnp.broadcast_to(cbb_ref[:, cs], (ts, LANES))
        for k in range(kb):
            off = CONV_B_HALO - (kb - 1) + k
            xc = xc + cbw_ref[k:k + 1, cs] * xr_buf[off:off + ts, cs]
        xcb = xc.astype(BF16)
        r = _sigmoid(jnp.dot(xcb, wa_ref[c], preferred_element_type=F32) + ba_ref[:, cs])
        gi = _sigmoid(jnp.dot(xcb, wx_ref[c], preferred_element_type=F32) + bx_ref[:, cs])
        log_a = (-RG_C) * r * _softplus(-lam_ref[:, cs])
        a = jnp.exp(log_a)
        uu = jnp.sqrt(1.0 - jnp.exp(2.0 * log_a)) * (gi * xc)
        d = 1
        while d < ts:
            a_prev = _shift_rows(a, d, 1.0)
            u_prev = _shift_rows(uu, d, 0.0)
            uu = a * u_prev + uu
            a = a * a_prev
            d *= 2
        h = uu + a * hc_ref[:, cs]
        hc_ref[:, cs] = h[ts - 1:ts, :]
        gr = u_ref[:, 3 * ch + c * LANES:3 * ch + (c + 1) * LANES]
        gelu = 0.5 * gr * (1.0 + jnp.tanh(math.sqrt(2.0 / math.pi) * (gr + 0.044715 * (gr * gr * gr))))
        o_ref[:, ch + c * LANES:ch + (c + 1) * LANES] = (h * gelu).astype(o_ref.dtype)
    xr_buf[0:CONV_B_HALO, :] = xr_buf[ts:ts + CONV_B_HALO, :]


def _recmix(u, caw, cab, lng, lnb, cbw, cbb, wa, ba, wx, bx, lam, layer, ts=256):
    s, w4 = u.shape
    ch = w4 // 4
    row = lambda v: v.reshape(1, ch)
    const2 = lambda i: (0, 0)
    gate_spec = pl.BlockSpec((None,) + wa.shape[1:], lambda i: (layer, 0, 0, 0))
    return pl.pallas_call(
        functools.partial(_recmix_kernel, ts=ts, ch=ch),
        grid=(s // ts,),
        in_specs=[
            pl.BlockSpec((ts, w4), lambda i: (i, 0)),
            pl.BlockSpec(caw.shape, const2),
            pl.BlockSpec((1, ch), const2),
            pl.BlockSpec((1, ch), const2),
            pl.BlockSpec((1, ch), const2),
            pl.BlockSpec(cbw.shape, const2),
            pl.BlockSpec((1, ch), const2),
            gate_spec,
            pl.BlockSpec((1, ch), const2),
            gate_spec,
            pl.BlockSpec((1, ch), const2),
            pl.BlockSpec((1, ch), const2),
        ],
        out_specs=pl.BlockSpec((ts, 2 * ch), lambda i: (i, 0)),
        out_shape=jax.ShapeDtypeStruct((s, 2 * ch), BF16),
        scratch_shapes=[
            pltpu.VMEM((ts + CONV_A_HALO, ch), F32),
            pltpu.VMEM((ts, ch), F32),
            pltpu.VMEM((ts + CONV_B_HALO, ch), F32),
            pltpu.VMEM((1, ch), F32),
        ],
        compiler_params=_params(("arbitrary",)),
        name="recmix",
    )(u, caw, row(cab), row(lng), row(lnb), cbw, row(cbb), wa, row(ba), wx, row(bx), row(lam))


ATTN_HEADS_PER_STEP = 2
ATTN_KEY_BLOCKS_PER_TRIP = 4
LOG2E = 1.4426950408889634
SIGN_BIT = -2147483648


def _attn_kernel(q_ref, k_ref, v_ref, ntri_ref, o_ref, acc_ref, carry_ref, *, bq, bk, dh):
    qi = pl.program_id(1)
    ntri = ntri_ref[...]
    nt = (((1,), (1,)), ((), ()))
    heads = range(ATTN_HEADS_PER_STEP)
    step = ATTN_KEY_BLOCKS_PER_TRIP

    def block_ref(ref, hh, kb):
        return ref[pl.ds(pl.multiple_of(kb * bk, bk), bk), hh * dh:(hh + 1) * dh]

    def sweep(kbs, masks=None, rows=slice(None)):
        masks = masks or [None] * len(kbs)
        zs = [[lax.dot_general(q_ref[rows, hh * dh:(hh + 1) * dh], block_ref(k_ref, hh, kb), nt,
                               preferred_element_type=F32) for kb in kbs] for hh in heads]
        xs, sums = [], []
        for hh in heads:
            newer = None
            row = []
            for n in range(len(kbs)):
                z = zs[hh][n]
                t = lax.bitcast_convert_type(lax.bitcast_convert_type(z, jnp.int32) | SIGN_BIT, F32)
                sp = jnp.maximum(z, 0.0) + jnp.log(1.0 + jnp.exp2(t)) * LOG2E
                if masks[n] is not None:
                    sp = jnp.where(masks[n], sp, 0.0)
                tail = jnp.dot(sp.astype(BF16), ntri, preferred_element_type=F32)
                x = z + tail
                if newer is not None:
                    x = x + newer
                row.append(x)
                newer = tail[:, 0:1] if newer is None else newer + tail[:, 0:1]
            xs.append(row)
            sums.append(newer)
        v_rows = pl.ds(pl.multiple_of(kbs[-1] * bk, bk), len(kbs) * bk)
        for hh in heads:
            ws = []
            for n in reversed(range(len(kbs))):
                w = jnp.exp2(xs[hh][n])
                if masks[n] is not None:
                    w = jnp.where(masks[n], w, 0.0)
                ws.append(w.astype(BF16))
            pv = jnp.dot(jnp.concatenate(ws, axis=1), v_ref[v_rows, hh * dh:(hh + 1) * dh],
                         preferred_element_type=F32)
            acc_ref[hh, rows] = acc_ref[hh, rows] + jnp.exp2(carry_ref[hh, rows]) * pv
            carry_ref[hh, rows] = carry_ref[hh, rows] + sums[hh]

    acc_ref[...] = jnp.zeros(acc_ref.shape, F32)
    carry_ref[...] = jnp.zeros(carry_ref.shape, F32)

    below_diagonal = (lax.broadcasted_iota(jnp.int32, (bk, bk), 1)
                      < lax.broadcasted_iota(jnp.int32, (bk, bk), 0))
    for r in range(step):
        sweep([step * qi + r - m for m in range(r + 1)], [below_diagonal] + [None] * r,
              rows=slice(r * bk, (r + 1) * bk))

    def older(j, _):
        newest = step * (qi - j) - 1
        sweep([newest - n for n in range(step)])
        return 0

    lax.fori_loop(0, qi, older, 0)

    for hh in heads:
        o_ref[:, hh * dh:(hh + 1) * dh] = acc_ref[hh].astype(o_ref.dtype)


def _attention(qkv, n_heads, bq=1024, bk=256):
    s, w3 = qkv.shape
    dh = w3 // (3 * n_heads)
    hps = ATTN_HEADS_PER_STEP
    groups = n_heads // hps
    assert bq == ATTN_KEY_BLOCKS_PER_TRIP * bk
    tri = (jnp.arange(bk)[:, None] >= jnp.arange(bk)[None, :])
    ntri = -tri.astype(BF16)
    return pl.pallas_call(
        functools.partial(_attn_kernel, bq=bq, bk=bk, dh=dh),
        grid=(groups, s // bq),
        in_specs=[
            pl.BlockSpec((bq, hps * dh), lambda g, i: (i, g)),
            pl.BlockSpec((s, hps * dh), lambda g, i: (0, groups + g)),
            pl.BlockSpec((s, hps * dh), lambda g, i: (0, 2 * groups + g)),
            pl.BlockSpec((bk, bk), lambda g, i: (0, 0)),
        ],
        out_specs=pl.BlockSpec((bq, hps * dh), lambda g, i: (i, g)),
        out_shape=jax.ShapeDtypeStruct((s, n_heads * dh), BF16),
        scratch_shapes=[pltpu.VMEM((hps, bq, dh), F32), pltpu.VMEM((hps, bq, 1), F32)],
        compiler_params=_params(("parallel", "arbitrary")),
        name="sb_attention",
    )(qkv, qkv, qkv, ntri)


def kernel(x, p, norm_mix_g, norm_mlp_g, norm_ple_g, norm_f_g, w_in_rec, conv_a_w, conv_a_b, ln_a_g, ln_a_b, conv_b_w, conv_b_b, w_rg_a, b_rg_a, w_rg_x, b_rg_x, rg_lambda, w_out_rec, w_qkv, w_o_attn, w_mlp_up, w_mlp_down, w_ple_proj, w_ple_gate):
    bsz, seq, d = x.shape
    depth = p.shape[0]
    outs = []
    w_in_rec, w_out_rec, w_qkv, w_o_attn, w_ple_proj, w_ple_gate, w_rg_a, w_rg_x = (
        w.astype(BF16) for w in (w_in_rec, w_out_rec, w_qkv, w_o_attn,
                                 w_ple_proj, w_ple_gate, w_rg_a, w_rg_x))
    q_scale = LOG2E / math.sqrt(d // SB_HEADS)
    qkv_scale = jnp.concatenate([jnp.full((d,), q_scale, F32), jnp.ones((2 * d,), F32)])
    for b in range(bsz):
        h = x[b]
        for i in range(depth):
            j = i // 2
            if i % 2 == 0:
                u = _norm_matmul(h, norm_mix_g[i], w_in_rec, j, F32)
                y = _recmix(u, conv_a_w[j], conv_a_b[j], ln_a_g[j], ln_a_b[j], conv_b_w[j], conv_b_b[j],
                            w_rg_a, b_rg_a[j], w_rg_x, b_rg_x[j], rg_lambda[j], j)
                h = _matmul_residual(y, w_out_rec, j, h)
            else:
                qkv = _norm_matmul(h, norm_mix_g[i], w_qkv, j, BF16, col_scale=qkv_scale)
                o = _attention(qkv, SB_HEADS)
                h = _matmul_residual(o, w_o_attn, j, h)
            h = _mlp(h, norm_mlp_g[i], w_mlp_up, w_mlp_down, i)
            h = _ple(h, norm_ple_g[i], w_ple_gate, p, w_ple_proj, i, b, norm_f_g,
                     final_norm=(i == depth - 1))
        outs.append(h)
    return jnp.stack(outs, axis=0)
```
